```python
import math
import jax, jax.numpy as jnp
from jax import lax
import numpy as np

D_MODEL = 1024
BATCH = 2
SEQ = 16384
DEPTH = 4

MIX_WIDTH = D_MODEL
S5_WIDTH = MIX_WIDTH // 2
POOL_WIDTH = MIX_WIDTH - S5_WIDTH
S5_GROUP = 16
S5_GROUPS = S5_WIDTH // S5_GROUP
S5_STATE = 64
DT_MIN = 0.001
DT_MAX = 0.1
POOL_WINDOWS = (2, 4, 8, 16)
POOL_GROUP = POOL_WIDTH // len(POOL_WINDOWS)
ATT_HEAD_DIM = 64
ATT_HEADS = D_MODEL // (2 * ATT_HEAD_DIM)
ATT_V_DIM = 2 * ATT_HEAD_DIM
QK_WIDTH = ATT_HEADS * 2 * ATT_HEAD_DIM
Q_BLOCK = 128
ROPE_THETA = 10000.0
N_EXPERTS = 16
EXPERT_FF = 2 * D_MODEL
CAPACITY_FACTOR = 2
EPS = 1e-6
N_EVEN = (DEPTH + 1) // 2
N_ODD = DEPTH // 2

kernel_name = 'hybrid_s5_pool_diffattn_ec_moe'

F32 = jnp.float32


def rmsnorm(x, g):
    xf = x.astype(F32)
    y = xf * lax.rsqrt(jnp.mean(xf * xf, axis=-1, keepdims=True) + EPS)
    return (y * g.astype(F32)).astype(x.dtype)


def rope_tables(positions, dim):
    inv = ROPE_THETA ** (-jnp.arange(0, dim, 2, dtype=F32) / dim)
    ang = positions.astype(F32)[..., None] * inv
    return jnp.cos(ang), jnp.sin(ang)


def apply_rope(x, cos, sin):
    x1, x2 = jnp.split(x, 2, axis=-1)
    c = cos[:, :, None, None, :]
    s = sin[:, :, None, None, :]
    return jnp.concatenate([x1 * c - x2 * s, x1 * s + x2 * c], axis=-1).astype(x.dtype)


def _ssm_combine(left, right):
    a_l, b_l = left
    a_r, b_r = right
    return a_r * a_l, a_r * b_l + b_r


def s5_direction(u, lam_re, lam_im, log_dt, b_re, b_im, c_re, c_im, reverse):
    lam = lax.complex(lam_re.astype(F32), lam_im.astype(F32))
    dt = jnp.exp(log_dt.astype(F32))[:, None]
    a_bar = jnp.exp(lam * dt)
    b = lax.complex(b_re.astype(F32), b_im.astype(F32))
    b_bar = ((a_bar - 1.0) / lam)[..., None] * b
    bu = jnp.einsum('bsgc,gpc->bsgp', u.astype(jnp.complex64), b_bar)
    a = jnp.broadcast_to(a_bar, bu.shape)
    _, h = lax.associative_scan(_ssm_combine, (a, bu), reverse=reverse, axis=1)
    c = lax.complex(c_re.astype(F32), c_im.astype(F32))
    return jnp.einsum('bsgp,gcp->bsgc', h, c).real


def s5_mixer(u, lam_re, lam_im, log_dt, b_re, b_im, c_re, c_im, d_skip, w_glu, b_glu):
    bsz, seq, _ = u.shape
    uf = u.astype(F32)
    ug = uf.reshape(bsz, seq, S5_GROUPS, S5_GROUP)
    y = d_skip.astype(F32) * uf
    for direction, rev in ((0, False), (1, True)):
        y = y + s5_direction(ug, lam_re[direction], lam_im[direction], log_dt[direction],
                             b_re[direction], b_im[direction], c_re[direction], c_im[direction],
                             rev).reshape(bsz, seq, S5_WIDTH)
    y = jax.nn.gelu(y)
    y = y * jax.nn.sigmoid(y @ w_glu.astype(F32) + b_glu.astype(F32))
    return y.astype(u.dtype)


def pool_mixer(v, w_group, scale):
    bsz, seq, _ = v.shape
    vf = v.astype(F32)
    cs = jnp.concatenate([jnp.zeros((bsz, 1, POOL_WIDTH), F32), lax.cumsum(vf, axis=1)], axis=1)
    t = jnp.arange(seq)
    outs = []
    for gi, win in enumerate(POOL_WINDOWS):
        lo = jnp.clip(t - win // 2, 0, seq)
        hi = jnp.clip(t + win // 2, 0, seq)
        sl = slice(gi * POOL_GROUP, (gi + 1) * POOL_GROUP)
        csg = cs[:, :, sl]
        cnt = (hi - lo).astype(F32)[None, :, None]
        outs.append((csg[:, hi] - csg[:, lo]) / cnt - vf[:, :, sl])
    pooled = jnp.stack(outs, axis=2)
    y = jnp.einsum('bsgc,gcd->bsgd', pooled, w_group.astype(F32)).reshape(bsz, seq, POOL_WIDTH)
    return (y * scale.astype(F32)).astype(v.dtype)


def diff_attention(h, cos, sin, w_qkv, w_out, q_norm_g, k_norm_g, lam_q1, lam_k1, lam_q2, lam_k2,
                   subln_g, layer_idx):
    bsz, seq, _ = h.shape
    nb = seq // Q_BLOCK
    qkv = h @ w_qkv
    q, k, v = jnp.split(qkv, [QK_WIDTH, 2 * QK_WIDTH], axis=-1)
    q = q.reshape(bsz, seq, ATT_HEADS, 2, ATT_HEAD_DIM)
    k = k.reshape(bsz, seq, ATT_HEADS, 2, ATT_HEAD_DIM)
    v = v.reshape(bsz, seq, ATT_HEADS, ATT_V_DIM)
    q = apply_rope(rmsnorm(q, q_norm_g), cos, sin)
    k = apply_rope(rmsnorm(k, k_norm_g), cos, sin)
    lam_init = 0.8 - 0.6 * math.exp(-0.3 * layer_idx)
    lam = (jnp.exp(jnp.sum(lam_q1.astype(F32) * lam_k1.astype(F32)))
           - jnp.exp(jnp.sum(lam_q2.astype(F32) * lam_k2.astype(F32))) + lam_init)
    qb = (q * (ATT_HEAD_DIM ** -0.5)).reshape(bsz, nb, Q_BLOCK, ATT_HEADS, 2, ATT_HEAD_DIM)
    qb = qb.transpose(1, 0, 3, 4, 2, 5)
    kt = k.transpose(0, 2, 3, 1, 4)
    vt = v.transpose(0, 2, 1, 3)

    def block(qblk):
        s = jnp.einsum('bhtqd,bhtkd->bhtqk', qblk, kt).astype(F32)
        p = jax.nn.softmax(s, axis=-1)
        p = p[:, :, 0] - lam * p[:, :, 1]
        return jnp.einsum('bhqk,bhkd->bhqd', p.astype(vt.dtype), vt)

    o = lax.map(block, qb)
    o = o.transpose(1, 0, 3, 2, 4).reshape(bsz, seq, ATT_HEADS, ATT_V_DIM)
    o = rmsnorm(o, subln_g) * (1.0 - lam_init)
    return o.reshape(bsz, seq, ATT_HEADS * ATT_V_DIM) @ w_out


def expert_choice_ffn(h, w_router, w_gate, w_up, w_down):
    bsz, seq, _ = h.shape
    cap = CAPACITY_FACTOR * seq // N_EXPERTS
    aff = jax.nn.softmax((h @ w_router).astype(F32), axis=-1)
    gate, idx = lax.top_k(aff.transpose(0, 2, 1), cap)
    b_idx = jnp.arange(bsz)[:, None, None]
    xg = h[b_idx, idx]
    a = jnp.einsum('becd,edf->becf', xg, w_gate)
    u = jnp.einsum('becd,edf->becf', xg, w_up)
    y = jnp.einsum('becf,efd->becd', jax.nn.silu(a) * u, w_down)
    y = y * gate[..., None].astype(y.dtype)
    return jnp.zeros_like(h).at[b_idx, idx].add(y)


def setup_inputs(seed: int = 0) -> dict:
    key = jax.random.key(seed)
    ks = jax.random.split(key, 32)
    nrm = lambda k, shape, std: jax.random.normal(k, shape, F32) * std
    out_scale = (2 * DEPTH) ** -0.5
    n_idx = jnp.arange(S5_STATE, dtype=F32)
    lam_re = -0.5 + nrm(ks[5], (N_EVEN, 2, S5_GROUPS, S5_STATE), 0.01)
    lam_im = jnp.pi * n_idx + nrm(ks[6], (N_EVEN, 2, S5_GROUPS, S5_STATE), 0.01)
    log_dt = jax.random.uniform(ks[7], (N_EVEN, 2, S5_GROUPS), F32, math.log(DT_MIN), math.log(DT_MAX))
    return {
        'x': nrm(ks[0], (BATCH, SEQ, D_MODEL), 1.0),
        'positions': jnp.broadcast_to(jnp.arange(SEQ, dtype=jnp.int32), (BATCH, SEQ)),
        'norm_mix_g': 1.0 + nrm(ks[1], (DEPTH, D_MODEL), 0.02),
        'norm_ffn_g': 1.0 + nrm(ks[2], (DEPTH, D_MODEL), 0.02),
        'hyb_w_in': nrm(ks[3], (N_EVEN, D_MODEL, MIX_WIDTH), D_MODEL ** -0.5),
        'hyb_w_out': nrm(ks[4], (N_EVEN, MIX_WIDTH, D_MODEL), MIX_WIDTH ** -0.5 * out_scale),
        's5_lam_re': lam_re,
        's5_lam_im': lam_im,
        's5_log_dt': log_dt,
        's5_b_re': nrm(ks[8], (N_EVEN, 2, S5_GROUPS, S5_STATE, S5_GROUP), (2 * S5_GROUP) ** -0.5),
        's5_b_im': nrm(ks[9], (N_EVEN, 2, S5_GROUPS, S5_STATE, S5_GROUP), (2 * S5_GROUP) ** -0.5),
        's5_c_re': nrm(ks[10], (N_EVEN, 2, S5_GROUPS, S5_GROUP, S5_STATE), (2 * S5_STATE) ** -0.5),
        's5_c_im': nrm(ks[11], (N_EVEN, 2, S5_GROUPS, S5_GROUP, S5_STATE), (2 * S5_STATE) ** -0.5),
        's5_d': nrm(ks[12], (N_EVEN, S5_WIDTH), 1.0),
        's5_w_glu': nrm(ks[13], (N_EVEN, S5_WIDTH, S5_WIDTH), S5_WIDTH ** -0.5),
        's5_b_glu': nrm(ks[14], (N_EVEN, S5_WIDTH), 0.01),
        'pool_w': nrm(ks[15], (N_EVEN, len(POOL_WINDOWS), POOL_GROUP, POOL_GROUP), POOL_GROUP ** -0.5),
        'pool_scale': 1.0 + nrm(ks[16], (N_EVEN, POOL_WIDTH), 0.02),
        'attn_w_qkv': nrm(ks[17], (N_ODD, D_MODEL, 2 * QK_WIDTH + ATT_HEADS * ATT_V_DIM), D_MODEL ** -0.5),
        'attn_w_out': nrm(ks[18], (N_ODD, ATT_HEADS * ATT_V_DIM, D_MODEL), (ATT_HEADS * ATT_V_DIM) ** -0.5 * out_scale),
        'attn_q_norm_g': 1.0 + nrm(ks[19], (N_ODD, ATT_HEAD_DIM), 0.02),
        'attn_k_norm_g': 1.0 + nrm(ks[20], (N_ODD, ATT_HEAD_DIM), 0.02),
        'attn_lam_q1': nrm(ks[21], (N_ODD, ATT_HEAD_DIM), 0.1),
        'attn_lam_k1': nrm(ks[22], (N_ODD, ATT_HEAD_DIM), 0.1),
        'attn_lam_q2': nrm(ks[23], (N_ODD, ATT_HEAD_DIM), 0.1),
        'attn_lam_k2': nrm(ks[24], (N_ODD, ATT_HEAD_DIM), 0.1),
        'attn_subln_g': 1.0 + nrm(ks[25], (N_ODD, ATT_V_DIM), 0.02),
        'moe_w_router': nrm(ks[26], (DEPTH, D_MODEL, N_EXPERTS), D_MODEL ** -0.5),
        'moe_w_gate': nrm(ks[27], (DEPTH, N_EXPERTS, D_MODEL, EXPERT_FF), D_MODEL ** -0.5),
        'moe_w_up': nrm(ks[28], (DEPTH, N_EXPERTS, D_MODEL, EXPERT_FF), D_MODEL ** -0.5),
        'moe_w_down': nrm(ks[29], (DEPTH, N_EXPERTS, EXPERT_FF, D_MODEL), EXPERT_FF ** -0.5 * out_scale),
    }


def reference(x, positions, norm_mix_g, norm_ffn_g, hyb_w_in, hyb_w_out, s5_lam_re, s5_lam_im,
              s5_log_dt, s5_b_re, s5_b_im, s5_c_re, s5_c_im, s5_d, s5_w_glu, s5_b_glu, pool_w,
              pool_scale, attn_w_qkv, attn_w_out, attn_q_norm_g, attn_k_norm_g, attn_lam_q1,
              attn_lam_k1, attn_lam_q2, attn_lam_k2, attn_subln_g, moe_w_router, moe_w_gate,
              moe_w_up, moe_w_down):
    cos, sin = rope_tables(positions, ATT_HEAD_DIM)
    h = x
    for layer in range(DEPTH):
        n = rmsnorm(h, norm_mix_g[layer])
        if layer % 2 == 0:
            e = layer // 2
            z = n @ hyb_w_in[e]
            ya = s5_mixer(z[..., :S5_WIDTH], s5_lam_re[e], s5_lam_im[e], s5_log_dt[e],
                          s5_b_re[e], s5_b_im[e], s5_c_re[e], s5_c_im[e], s5_d[e],
                          s5_w_glu[e], s5_b_glu[e])
            yb = pool_mixer(z[..., S5_WIDTH:], pool_w[e], pool_scale[e])
            mix = jnp.concatenate([ya, yb], axis=-1) @ hyb_w_out[e]
        else:
            o = layer // 2
            mix = diff_attention(n, cos, sin, attn_w_qkv[o], attn_w_out[o], attn_q_norm_g[o],
                                 attn_k_norm_g[o], attn_lam_q1[o], attn_lam_k1[o], attn_lam_q2[o],
                                 attn_lam_k2[o], attn_subln_g[o], layer)
        h = h + mix
        h = h + expert_choice_ffn(rmsnorm(h, norm_ffn_g[layer]), moe_w_router[layer],
                                  moe_w_gate[layer], moe_w_up[layer], moe_w_down[layer])
    return h
```

```python
import functools
import math

import jax
import jax.numpy as jnp
from jax import lax
from jax.experimental import pallas as pl
from jax.experimental.pallas import tpu as pltpu

F32 = jnp.float32
BF16 = jnp.bfloat16

EPS = 1e-6
LANES = 128
VMEM_LIMIT = 56 * 1024 * 1024

S5_GROUP = 16
S5_CHUNK = 16
POOL_WINDOWS = (2, 4, 8, 16)
POOL_HALO = 8
ATT_HEAD_DIM = 64
ROPE_THETA = 10000.0
N_EXPERTS = 16
CAPACITY_FACTOR = 2


def _row_tile(n, want):
    t = min(n, want)
    assert n % t == 0, (n, t)
    return t


def _params(*sem):
    return pltpu.CompilerParams(dimension_semantics=sem, vmem_limit_bytes=VMEM_LIMIT)


def _rmsnorm_rows(x, g):
    return x * lax.rsqrt(jnp.mean(x * x, axis=-1, keepdims=True) + EPS) * g


def _split_bf16(x):
    hi = x.astype(BF16)
    lo = (x - hi.astype(F32)).astype(BF16)
    return hi, lo


def _norm_proj_kernel(h_ref, g_ref, w_ref, o_ref):
    y = _rmsnorm_rows(h_ref[...], g_ref[...])
    o_ref[...] = jnp.dot(y.astype(BF16), w_ref[...], preferred_element_type=F32).astype(o_ref.dtype)


def norm_proj(h, g, w, out_dtype):
    t, d = h.shape
    n = w.shape[1]
    tm = _row_tile(t, 512)
    return pl.pallas_call(
        _norm_proj_kernel,
        out_shape=jax.ShapeDtypeStruct((t, n), out_dtype),
        grid=(t // tm,),
        in_specs=[pl.BlockSpec((tm, d), lambda i: (i, 0)),
                  pl.BlockSpec((1, d), lambda i: (0, 0)),
                  pl.BlockSpec((d, n), lambda i: (0, 0))],
        out_specs=pl.BlockSpec((tm, n), lambda i: (i, 0)),
        compiler_params=_params("parallel"),
        name="norm_proj",
    )(h, g.reshape(1, d), w)


def _s5_kernel(u_ref, wt_ref, wst_ref, wout_ref, al_ref, y_ref, s_ref, hf_ref, hb_ref, *, n_batch):
    nc_all = u_ref.shape[1]
    nc = nc_all // n_batch
    u = u_ref[0]
    s_ref[...] = jnp.dot(u, wst_ref[0], preferred_element_type=F32)
    al = al_ref[0]
    a_re = al[0:1, :]
    a_im = al[1:2, :]
    is_fwd = lax.broadcasted_iota(jnp.int32, (1, 2 * LANES), 1) % LANES < LANES // 2

    def step(i, carry):
        new = []
        for b in range(n_batch):
            h = carry[b]
            row_f = b * nc + i
            row_b = b * nc + nc - 1 - i
            hf_ref[pl.ds(row_f, 1), :] = h
            hb_ref[pl.ds(row_b, 1), :] = h
            s = jnp.where(is_fwd, s_ref[pl.ds(row_f, 1), :], s_ref[pl.ds(row_b, 1), :])
            h_re = h[:, :LANES]
            h_im = h[:, LANES:]
            n_re = a_re * h_re - a_im * h_im + s[:, :LANES]
            n_im = a_re * h_im + a_im * h_re + s[:, LANES:]
            new.append(jnp.concatenate([n_re, n_im], axis=1))
        return tuple(new)

    zero = jnp.zeros((1, 2 * LANES), F32)
    lax.fori_loop(0, nc, step, tuple(zero for _ in range(n_batch)))
    h_in = jnp.where(is_fwd, hf_ref[...], hb_ref[...])
    y = jnp.dot(u, wt_ref[0], preferred_element_type=F32)
    y = y + jnp.dot(h_in.astype(BF16), wout_ref[0], preferred_element_type=F32)
    y_ref[0] = y.astype(y_ref.dtype)


def s5_scan(u, wt, wst, wout, al, n_batch):
    g, nc_all, w = u.shape
    kern = functools.partial(_s5_kernel, n_batch=n_batch)
    mat = pl.BlockSpec((1, w, w), lambda i: (i, 0, 0))
    return pl.pallas_call(
        kern,
        out_shape=jax.ShapeDtypeStruct((g, nc_all, w), BF16),
        grid=(g,),
        in_specs=[pl.BlockSpec((1, nc_all, w), lambda i: (i, 0, 0)), mat, mat, mat,
                  pl.BlockSpec((1, 2, LANES), lambda i: (i, 0, 0))],
        out_specs=pl.BlockSpec((1, nc_all, w), lambda i: (i, 0, 0)),
        scratch_shapes=[pltpu.VMEM((nc_all, w), F32), pltpu.VMEM((nc_all, w), F32),
                        pltpu.VMEM((nc_all, w), F32)],
        compiler_params=_params("parallel"),
        name="s5_scan",
    )(u, wt, wst, wout, al)


def s5_chunk_matrices(lam_re, lam_im, log_dt, b_re, b_im, c_re, c_im):
    L = S5_CHUNK
    lam = lax.complex(lam_re.astype(F32), lam_im.astype(F32))
    dt = jnp.exp(log_dt.astype(F32))[..., None]
    lam_dt = lam * dt
    a_bar = jnp.exp(lam_dt)
    b = lax.complex(b_re.astype(F32), b_im.astype(F32))
    b_bar = ((a_bar - 1.0) / lam)[..., None] * b
    c = lax.complex(c_re.astype(F32), c_im.astype(F32))
    k = jnp.arange(L + 1, dtype=F32)
    pw = jnp.exp(lam_dt[:, :, None, :] * k[None, None, :, None])
    n_grp, n_state = lam.shape[1], lam.shape[2]
    kern = jnp.einsum('dgop,dgkp,dgpi->dgkoi', c, pw[:, :, :L], b_bar).real
    t_idx = jnp.arange(L)[None, :]
    s_idx = jnp.arange(L)[:, None]
    lag_f = t_idx - s_idx
    kf = jnp.where((lag_f >= 0)[None, :, :, None, None], kern[0][:, jnp.clip(lag_f, 0, L - 1)], 0.0)
    kb = jnp.where((lag_f <= 0)[None, :, :, None, None], kern[1][:, jnp.clip(-lag_f, 0, L - 1)], 0.0)
    wt = (kf + kb).transpose(0, 1, 4, 2, 3).reshape(n_grp, L * S5_GROUP, L * S5_GROUP)
    cf = pw[0][:, L - 1 - jnp.arange(L)][..., None] * b_bar[0][:, None]
    cb = pw[1][:, jnp.arange(L)][..., None] * b_bar[1][:, None]
    def st(x):
        return x.transpose(0, 1, 3, 2).reshape(n_grp, L * S5_GROUP, n_state)
    wst = jnp.concatenate([st(cf.real), st(cb.real), st(cf.imag), st(cb.imag)], axis=-1)
    of = c[0][:, None] * pw[0][:, 1 + jnp.arange(L)][:, :, None, :]
    ob = c[1][:, None] * pw[1][:, L - jnp.arange(L)][:, :, None, :]
    def ot(x):
        return x.transpose(0, 3, 1, 2).reshape(n_grp, n_state, L * S5_GROUP)
    wout = jnp.concatenate([ot(of.real), ot(ob.real), ot(-of.imag), ot(-ob.imag)], axis=1)
    a_l = pw[:, :, L]
    al = jnp.stack([jnp.concatenate([a_l[0].real, a_l[1].real], axis=-1),
                    jnp.concatenate([a_l[0].imag, a_l[1].imag], axis=-1)], axis=1)
    return wt, wst, wout, al


def _gelu_tanh(x):
    return 0.5 * x * (1.0 + jnp.tanh(math.sqrt(2.0 / math.pi) * (x + 0.044715 * (x * x * x))))


def _even_post_kernel(z_ref, zp_ref, zn_ref, ys_ref, d_ref, wglu_ref, bglu_ref, pw_ref, ps_ref, wout_ref,
                      h_ref, o_ref, o2_ref, ext_ref, cat_ref, *, seq):
    tm = z_ref.shape[0]
    half = z_ref.shape[1] // 2
    t0 = (pl.program_id(0) * tm) % seq
    y = d_ref[...] * z_ref[:, :half] + ys_ref[...].astype(F32)
    y = _gelu_tanh(y)
    gate = jnp.dot(y.astype(BF16), wglu_ref[...], preferred_element_type=F32) + bglu_ref[...]
    cat_ref[:, :half] = (y * (1.0 / (1.0 + jnp.exp(-gate)))).astype(BF16)
    v = z_ref[:, half:]
    ext_ref[0:POOL_HALO, :] = jnp.where(t0 > 0, zp_ref[...], 0.0)
    ext_ref[POOL_HALO:POOL_HALO + tm, :] = v
    ext_ref[POOL_HALO + tm:, :] = jnp.where(t0 + tm < seq, zn_ref[...], 0.0)
    pos = t0 + lax.broadcasted_iota(jnp.int32, (tm, 1), 0)
    grp = half // len(POOL_WINDOWS)
    for gi, win in enumerate(POOL_WINDOWS):
        cols = slice(gi * grp, (gi + 1) * grp)
        acc = ext_ref[POOL_HALO - win // 2:POOL_HALO - win // 2 + tm, cols]
        for j in range(1, win):
            off = POOL_HALO - win // 2 + j
            acc = acc + ext_ref[off:off + tm, cols]
        cnt = jnp.minimum(pos + win // 2, seq) - jnp.maximum(pos - win // 2, 0)
        pooled = acc / cnt.astype(F32) - v[:, cols]
        yb = jnp.dot(pooled.astype(BF16), pw_ref[gi], preferred_element_type=F32) * ps_ref[:, cols]
        cat_ref[:, half + gi * grp:half + (gi + 1) * grp] = yb.astype(BF16)
    out = h_ref[...] + jnp.dot(cat_ref[...], wout_ref[...], preferred_element_type=F32)
    o_ref[...] = out
    o2_ref[...] = out


def even_post(z, ys, d_skip, w_glu, b_glu, pool_w, pool_scale, w_out, h, seq):
    t, width = z.shape
    half = width // 2
    d = h.shape[1]
    tm = _row_tile(seq, 512)
    hb = tm // POOL_HALO
    nblk8 = t // POOL_HALO
    kern = functools.partial(_even_post_kernel, seq=seq)
    full = lambda shape: pl.BlockSpec(shape, lambda i: (0,) * len(shape))
    return pl.pallas_call(
        kern,
        out_shape=(jax.ShapeDtypeStruct((t, d), F32), jax.ShapeDtypeStruct((t, d), F32)),
        grid=(t // tm,),
        in_specs=[pl.BlockSpec((tm, width), lambda i: (i, 0)),
                  pl.BlockSpec((POOL_HALO, half), lambda i: (jnp.maximum(i * hb - 1, 0), 1)),
                  pl.BlockSpec((POOL_HALO, half), lambda i: (jnp.minimum((i + 1) * hb, nblk8 - 1), 1)),
                  pl.BlockSpec((tm, half), lambda i: (i, 0)),
                  full((1, half)), full((half, half)), full((1, half)),
                  full(pool_w.shape), full((1, half)), full((width, d)),
                  pl.BlockSpec((tm, d), lambda i: (i, 0))],
        out_specs=(pl.BlockSpec((tm, d), lambda i: (i, 0)), pl.BlockSpec((tm, d), lambda i: (i, 0))),
        scratch_shapes=[pltpu.VMEM((tm + 2 * POOL_HALO, half), F32), pltpu.VMEM((tm, width), BF16)],
        compiler_params=_params("parallel"),
        name="even_post",
    )(z, z, z, ys, d_skip.reshape(1, half), w_glu, b_glu.reshape(1, half), pool_w,
      pool_scale.reshape(1, half), w_out, h)


def _rope_kernel(pos_ref, inv_ref, sign_ref, cos_ref, sin_ref):
    ang = pos_ref[...].astype(F32) * inv_ref[...]
    cos_ref[...] = jnp.cos(ang)
    sin_ref[...] = jnp.sin(ang) * sign_ref[...]


def rope_tables(positions):
    t = positions.size
    half = ATT_HEAD_DIM // 2
    inv = ROPE_THETA ** (-jnp.arange(0, ATT_HEAD_DIM, 2, dtype=F32) / ATT_HEAD_DIM)
    reps = LANES // half
    inv_pat = jnp.tile(inv, reps).reshape(1, LANES)
    sign = jnp.tile(jnp.concatenate([-jnp.ones((half,), F32), jnp.ones((half,), F32)]), reps // 2).reshape(1, LANES)
    pos = jnp.broadcast_to(positions.reshape(t, 1), (t, LANES))
    tm = _row_tile(t, 1024)
    return pl.pallas_call(
        _rope_kernel,
        out_shape=(jax.ShapeDtypeStruct((t, LANES), F32), jax.ShapeDtypeStruct((t, LANES), F32)),
        grid=(t // tm,),
        in_specs=[pl.BlockSpec((tm, LANES), lambda i: (i, 0)),
                  pl.BlockSpec((1, LANES), lambda i: (0, 0)),
                  pl.BlockSpec((1, LANES), lambda i: (0, 0))],
        out_specs=(pl.BlockSpec((tm, LANES), lambda i: (i, 0)), pl.BlockSpec((tm, LANES), lambda i: (i, 0))),
        compiler_params=_params("parallel"),
        name="rope_tables",
    )(pos, inv_pat, sign)


def _qkv_kernel(h_ref, g_ref, w_ref, cos_ref, sin_ref, qg_ref, kg_ref, seg_ref, q_ref, k_ref, v_ref):
    d = h_ref.shape[1]
    y = _rmsnorm_rows(h_ref[...], g_ref[...]).astype(BF16)
    cos = cos_ref[...]
    sin = sin_ref[...]
    seg = seg_ref[...]
    first_half = lax.broadcasted_iota(jnp.int32, (1, LANES), 1) % ATT_HEAD_DIM < ATT_HEAD_DIM // 2
    for which, (o_ref, gain_ref, scale) in enumerate(((q_ref, qg_ref, ATT_HEAD_DIM ** -0.5), (k_ref, kg_ref, 1.0))):
        x_all = jnp.dot(y, w_ref[:, which * d:(which + 1) * d], preferred_element_type=F32)
        for j in range(d // LANES):
            x = x_all[:, j * LANES:(j + 1) * LANES]
            hi, lo = _split_bf16(x * x)
            ss = jnp.dot(hi, seg, preferred_element_type=F32) + jnp.dot(lo, seg, preferred_element_type=F32)
            xn = x * lax.rsqrt(ss * (1.0 / ATT_HEAD_DIM) + EPS) * gain_ref[...]
            swapped = jnp.where(first_half, pltpu.roll(xn, LANES - ATT_HEAD_DIM // 2, 1),
                                pltpu.roll(xn, ATT_HEAD_DIM // 2, 1))
            o_ref[:, j * LANES:(j + 1) * LANES] = ((xn * cos + swapped * sin) * scale).astype(BF16)
    v_ref[...] = jnp.dot(y, w_ref[:, 2 * d:], preferred_element_type=F32).astype(BF16)


def attn_qkv(h, g, w_qkv, cos, sin, q_gain, k_gain):
    t, d = h.shape
    tm = _row_tile(t, 256)
    reps = LANES // ATT_HEAD_DIM
    lane = jnp.arange(LANES)
    seg = (lane[:, None] // ATT_HEAD_DIM == lane[None, :] // ATT_HEAD_DIM).astype(BF16)
    row = lambda n: pl.BlockSpec((tm, n), lambda i: (i, 0))
    one = lambda n: pl.BlockSpec((1, n), lambda i: (0, 0))
    out = jax.ShapeDtypeStruct((t, d), BF16)
    return pl.pallas_call(
        _qkv_kernel,
        out_shape=(out, out, out),
        grid=(t // tm,),
        in_specs=[row(d), one(d), pl.BlockSpec(w_qkv.shape, lambda i: (0, 0)), row(LANES), row(LANES),
                  one(LANES), one(LANES), pl.BlockSpec((LANES, LANES), lambda i: (0, 0))],
        out_specs=(row(d), row(d), row(d)),
        compiler_params=_params("parallel"),
        name="attn_qkv",
    )(h, g.reshape(1, d), w_qkv, cos, sin, jnp.tile(q_gain.astype(F32), reps).reshape(1, LANES),
      jnp.tile(k_gain.astype(F32), reps).reshape(1, LANES), seg)


def _flash_kernel(q_ref, k_ref, v_ref, lq1_ref, lk1_ref, lq2_ref, lk2_ref, sg_ref, o_ref,
                  qm_ref, m_ref, l_ref, acc_ref, *, lam_init):
    tq = q_ref.shape[0]
    ki = pl.program_id(3)

    @pl.when(ki == 0)
    def _():
        q = q_ref[...]
        first = lax.broadcasted_iota(jnp.int32, (1, LANES), 1) < ATT_HEAD_DIM
        zero = jnp.zeros_like(q)
        qm_ref[0:tq, :] = jnp.where(first, q, zero)
        qm_ref[tq:, :] = jnp.where(first, zero, q)
        m_ref[...] = jnp.full(m_ref.shape, -jnp.inf, F32)
        l_ref[...] = jnp.zeros(l_ref.shape, F32)
        acc_ref[...] = jnp.zeros(acc_ref.shape, F32)

    s = lax.dot_general(qm_ref[...], k_ref[...], (((1,), (1,)), ((), ())), preferred_element_type=F32)
    m_prev = m_ref[...]
    m_new = jnp.maximum(m_prev, jnp.max(s, axis=-1, keepdims=True))
    alpha = jnp.exp(m_prev - m_new)
    p = jnp.exp(s - m_new)
    l_ref[...] = alpha * l_ref[...] + jnp.sum(p, axis=-1, keepdims=True)
    acc_ref[...] = alpha * acc_ref[...] + jnp.dot(p.astype(BF16), v_ref[...], preferred_element_type=F32)
    m_ref[...] = m_new

    @pl.when(ki == pl.num_programs(3) - 1)
    def _():
        lam = (jnp.exp(jnp.sum(lq1_ref[...] * lk1_ref[...], axis=-1, keepdims=True))
               - jnp.exp(jnp.sum(lq2_ref[...] * lk2_ref[...], axis=-1, keepdims=True)) + lam_init)
        o = acc_ref[...] / l_ref[...]
        o = o[0:tq, :] - lam * o[tq:, :]
        o = o * lax.rsqrt(jnp.mean(o * o, axis=-1, keepdims=True) + EPS) * sg_ref[...]
        o_ref[...] = (o * (1.0 - lam_init)).astype(o_ref.dtype)


def flash_diff_attention(q, k, v, lam_q1, lam_k1, lam_q2, lam_k2, subln_g, n_batch, lam_init):
    t, width = q.shape
    seq = t // n_batch
    heads = width // LANES
    tq = _row_tile(seq, 256)
    tk = _row_tile(seq, 512)
    nq, nk = seq // tq, seq // tk
    kern = functools.partial(_flash_kernel, lam_init=lam_init)
    vec = lambda n: pl.BlockSpec((1, n), lambda b, h, qi, ki: (0, 0))
    lam_vec = lambda x: x.astype(F32).reshape(1, ATT_HEAD_DIM)
    return pl.pallas_call(
        kern,
        out_shape=jax.ShapeDtypeStruct((t, width), BF16),
        grid=(n_batch, heads, nq, nk),
        in_specs=[pl.BlockSpec((tq, LANES), lambda b, h, qi, ki: (b * nq + qi, h)),
                  pl.BlockSpec((tk, LANES), lambda b, h, qi, ki: (b * nk + ki, h)),
                  pl.BlockSpec((tk, LANES), lambda b, h, qi, ki: (b * nk + ki, h)),
                  vec(ATT_HEAD_DIM), vec(ATT_HEAD_DIM), vec(ATT_HEAD_DIM), vec(ATT_HEAD_DIM), vec(LANES)],
        out_specs=pl.BlockSpec((tq, LANES), lambda b, h, qi, ki: (b * nq + qi, h)),
        scratch_shapes=[pltpu.VMEM((2 * tq, LANES), BF16), pltpu.VMEM((2 * tq, 1), F32),
                        pltpu.VMEM((2 * tq, 1), F32), pltpu.VMEM((2 * tq, LANES), F32)],
        compiler_params=_params("parallel", "parallel", "parallel", "arbitrary"),
        name="flash_diff_attention",
    )(q, k, v, lam_vec(lam_q1), lam_vec(lam_k1), lam_vec(lam_q2), lam_vec(lam_k2),
      subln_g.astype(F32).reshape(1, LANES))


def _proj_residual_kernel(x_ref, w_ref, h_ref, o_ref, o2_ref):
    out = h_ref[...] + jnp.dot(x_ref[...], w_ref[...], preferred_element_type=F32)
    o_ref[...] = out
    o2_ref[...] = out


def proj_residual(x, w, h):
    t, d = h.shape
    kdim = x.shape[1]
    tm = _row_tile(t, 512)
    row = lambda n: pl.BlockSpec((tm, n), lambda i: (i, 0))
    out = jax.ShapeDtypeStruct((t, d), F32)
    return pl.pallas_call(
        _proj_residual_kernel,
        out_shape=(out, out),
        grid=(t // tm,),
        in_specs=[row(kdim), pl.BlockSpec((kdim, d), lambda i: (0, 0)), row(d)],
        out_specs=(row(d), row(d)),
        compiler_params=_params("parallel"),
        name="proj_residual",
    )(x, w, h)


def _router_kernel(h_ref, g_ref, wt_hi_ref, wt_lo_ref, aff_ref):
    y = _rmsnorm_rows(h_ref[...], g_ref[...])
    y_hi, y_lo = _split_bf16(y)
    nt = (((1,), (1,)), ((), ()))
    logits = (lax.dot_general(wt_hi_ref[...], y_hi, nt, preferred_element_type=F32)
              + lax.dot_general(wt_hi_ref[...], y_lo, nt, preferred_element_type=F32)
              + lax.dot_general(wt_lo_ref[...], y_hi, nt, preferred_element_type=F32))
    e = jnp.exp(logits - jnp.max(logits, axis=0, keepdims=True))
    aff_ref[0] = e / jnp.sum(e, axis=0, keepdims=True)


def router(h, g, w_router, n_batch):
    t, d = h.shape
    seq = t // n_batch
    n_exp = w_router.shape[1]
    tm = _row_tile(seq, 512)
    ns = seq // tm
    wt = w_router.astype(F32).T
    wt_hi, wt_lo = _split_bf16(wt)
    return pl.pallas_call(
        _router_kernel,
        out_shape=jax.ShapeDtypeStruct((n_batch, n_exp, seq), F32),
        grid=(n_batch, ns),
        in_specs=[pl.BlockSpec((tm, d), lambda b, i: (b * ns + i, 0)),
                  pl.BlockSpec((1, d), lambda b, i: (0, 0)),
                  pl.BlockSpec((n_exp, d), lambda b, i: (0, 0)),
                  pl.BlockSpec((n_exp, d), lambda b, i: (0, 0))],
        out_specs=pl.BlockSpec((1, n_exp, tm), lambda b, i: (b, 0, i)),
        compiler_params=_params("parallel", "parallel"),
        name="router",
    )(h, g.reshape(1, d), wt_hi, wt_lo)


def _select_kernel(aff_ref, affw_ref, idx_ref, gate_ref, *, cap):
    rows_all = aff_ref.shape[1]
    n_exp = idx_ref.shape[1]
    r = rows_all // n_exp
    aff = aff_ref[0]
    bits_wide = pltpu.bitcast(affw_ref[0], jnp.int32)

    def count(mask):
        return jnp.sum(jnp.where(mask, 1.0, 0.0), axis=1, keepdims=True)

    def bit_step(i, thr):
        cand = thr | jnp.left_shift(jnp.int32(1), 30 - i)
        return jnp.where(count(bits_wide >= cand) >= cap, cand, thr)

    thr = lax.fori_loop(0, 31, bit_step, jnp.zeros((n_exp, 1), jnp.int32))
    n_tie_take = cap - count(bits_wide > thr)

    tri_incl = (lax.broadcasted_iota(jnp.int32, (LANES, LANES), 0)
                <= lax.broadcasted_iota(jnp.int32, (LANES, LANES), 1)).astype(BF16)
    ones_mat = jnp.ones((LANES, LANES), BF16)
    low_strict = (lax.broadcasted_iota(jnp.int32, (r, r), 1)
                  < lax.broadcasted_iota(jnp.int32, (r, r), 0)).astype(BF16)

    def prefix(mask2d):
        m = jnp.where(mask2d, 1.0, 0.0).astype(BF16)
        lane_incl = jnp.dot(m, tri_incl, preferred_element_type=F32)
        row_tot = jnp.dot(m, ones_mat, preferred_element_type=F32)
        row_off = jnp.dot(low_strict, row_tot.astype(BF16), preferred_element_type=F32)
        return lane_incl, row_tot, row_off

    slot = lax.broadcasted_iota(jnp.int32, (1, cap), 1).astype(F32)
    row_id = lax.broadcasted_iota(jnp.int32, (r, cap), 0).astype(F32)
    lane_id = lax.broadcasted_iota(jnp.int32, (LANES, cap), 0).astype(F32)
    tn = (((0,), (0,)), ((), ()))
    for e in range(n_exp):
        a_e = aff[e * r:(e + 1) * r, :]
        bits = pltpu.bitcast(a_e, jnp.int32)
        thr_e = thr[e:e + 1, :]
        tie_e = bits == thr_e
        t_incl, _, t_off = prefix(tie_e)
        tie_rank = t_off + t_incl - 1.0
        take = jnp.where(tie_rank < n_tie_take[e:e + 1, :], 1.0, 0.0) * jnp.where(tie_e, 1.0, 0.0)
        sel = jnp.where(bits > thr_e, 1.0, take) > 0.5
        lane_incl, row_tot, row_off = prefix(sel)
        off_b = jnp.tile(row_off, (1, cap // LANES))
        cum_b = jnp.tile(row_off + row_tot, (1, cap // LANES))
        row_of = jnp.sum(jnp.where(cum_b <= slot, 1.0, 0.0), axis=0, keepdims=True)
        off_of = jnp.max(jnp.where(off_b <= slot, off_b, 0.0), axis=0, keepdims=True)
        local = slot - off_of
        onehot = jnp.where(row_id == row_of, 1.0, 0.0).astype(BF16)
        incl_t = lax.dot_general(lane_incl.astype(BF16), onehot, tn, preferred_element_type=F32)
        lane_of = jnp.sum(jnp.where(incl_t <= local, 1.0, 0.0), axis=0, keepdims=True)
        idx_ref[0, e:e + 1, :] = (row_of * LANES + lane_of).astype(jnp.int32)
        a_hi = a_e.astype(BF16)
        a_mid = (a_e - a_hi.astype(F32)).astype(BF16)
        a_lo = (a_e - a_hi.astype(F32) - a_mid.astype(F32)).astype(BF16)
        a_t = (lax.dot_general(a_hi, onehot, tn, preferred_element_type=F32)
               + lax.dot_general(a_mid, onehot, tn, preferred_element_type=F32)
               + lax.dot_general(a_lo, onehot, tn, preferred_element_type=F32))
        gate_ref[0, e:e + 1, :] = jnp.sum(jnp.where(lane_id == lane_of, a_t, 0.0), axis=0, keepdims=True)


def expert_select(aff_t, cap):
    n_batch, n_exp, seq = aff_t.shape
    rows = n_exp * seq // LANES
    kern = functools.partial(_select_kernel, cap=cap)
    return pl.pallas_call(
        kern,
        out_shape=(jax.ShapeDtypeStruct((n_batch, n_exp, cap), jnp.int32),
                   jax.ShapeDtypeStruct((n_batch, n_exp, cap), F32)),
        grid=(n_batch,),
        in_specs=[pl.BlockSpec((1, rows, LANES), lambda b: (b, 0, 0)),
                  pl.BlockSpec((1, n_exp, seq), lambda b: (b, 0, 0))],
        out_specs=(pl.BlockSpec((1, n_exp, cap), lambda b: (b, 0, 0)),
                   pl.BlockSpec((1, n_exp, cap), lambda b: (b, 0, 0))),
        compiler_params=_params("parallel"),
        name="expert_select",
    )(aff_t.reshape(n_batch, rows, LANES), aff_t)


def _moe_kernel(idx_ref, gate_ref, g_ref, wg_ref, wu_ref, wd_ref, hin_ref, acc_in_ref, acc_ref,
                x_buf, o_buf, sem, *, seq, cap):
    del acc_in_ref
    e, b, ti = pl.program_id(0), pl.program_id(1), pl.program_id(2)
    tm = x_buf.shape[0]
    n_exp = pl.num_programs(0)
    base = (b * n_exp + e) * cap + ti * tm
    row0 = b * seq

    def rows(j):
        return pl.ds(row0 + idx_ref[base + j], 1)

    def start_gather(j, c):
        pltpu.make_async_copy(hin_ref.at[rows(j)], x_buf.at[pl.ds(j, 1)], sem.at[0]).start()
        pltpu.make_async_copy(acc_ref.at[rows(j)], o_buf.at[pl.ds(j, 1)], sem.at[1]).start()
        return c

    lax.fori_loop(0, tm, start_gather, 0)
    pltpu.make_async_copy(hin_ref.at[pl.ds(0, tm)], x_buf, sem.at[0]).wait()
    pltpu.make_async_copy(acc_ref.at[pl.ds(0, tm)], o_buf, sem.at[1]).wait()

    x = _rmsnorm_rows(x_buf[...], g_ref[...]).astype(BF16)
    a = jnp.dot(x, wg_ref[0], preferred_element_type=F32)
    u = jnp.dot(x, wu_ref[0], preferred_element_type=F32)
    hmid = (a * (1.0 / (1.0 + jnp.exp(-a))) * u).astype(BF16)
    y = jnp.dot(hmid, wd_ref[0], preferred_element_type=F32)
    o_buf[...] = o_buf[...] + y * gate_ref[...]

    def start_scatter(j, c):
        pltpu.make_async_copy(o_buf.at[pl.ds(j, 1)], acc_ref.at[rows(j)], sem.at[2]).start()
        return c

    lax.fori_loop(0, tm, start_scatter, 0)
    pltpu.make_async_copy(o_buf, acc_ref.at[pl.ds(0, tm)], sem.at[2]).wait()


def moe_ffn(h, acc, g, idx, gate, w_gate, w_up, w_down, n_batch):
    t, d = h.shape
    seq = t // n_batch
    n_exp, _, ff = w_gate.shape
    cap = idx.shape[-1]
    tm = _row_tile(cap, 512)
    nt = cap // tm
    kern = functools.partial(_moe_kernel, seq=seq, cap=cap)
    grid_spec = pltpu.PrefetchScalarGridSpec(
        num_scalar_prefetch=1,
        grid=(n_exp, n_batch, nt),
        in_specs=[pl.BlockSpec((tm, 1), lambda e, b, i, idx: ((b * n_exp + e) * nt + i, 0)),
                  pl.BlockSpec((1, d), lambda e, b, i, idx: (0, 0)),
                  pl.BlockSpec((1, d, ff), lambda e, b, i, idx: (e, 0, 0)),
                  pl.BlockSpec((1, d, ff), lambda e, b, i, idx: (e, 0, 0)),
                  pl.BlockSpec((1, ff, d), lambda e, b, i, idx: (e, 0, 0)),
                  pl.BlockSpec(memory_space=pl.ANY),
                  pl.BlockSpec(memory_space=pl.ANY)],
        out_specs=pl.BlockSpec(memory_space=pl.ANY),
        scratch_shapes=[pltpu.VMEM((tm, d), F32), pltpu.VMEM((tm, d), F32), pltpu.SemaphoreType.DMA((3,))],
    )
    return pl.pallas_call(
        kern,
        out_shape=jax.ShapeDtypeStruct((t, d), F32),
        grid_spec=grid_spec,
        input_output_aliases={7: 0},
        compiler_params=_params("arbitrary", "arbitrary", "arbitrary"),
        name="moe_ffn",
    )(idx.reshape(-1), gate.reshape(-1, 1), g.reshape(1, d), w_gate, w_up, w_down, h, acc)


def _even_layer(h, n_batch, seq, g_mix, w_in, w_out, s5, d_skip, w_glu, b_glu, pool_w, pool_scale):
    t = h.shape[0]
    z = norm_proj(h, g_mix, w_in.astype(BF16), F32)
    half = z.shape[1] // 2
    n_grp = half // S5_GROUP
    nc = t // S5_CHUNK
    u = z[:, :half].astype(BF16).reshape(nc, S5_CHUNK, n_grp, S5_GROUP)
    u = u.transpose(2, 0, 1, 3).reshape(n_grp, nc, S5_CHUNK * S5_GROUP)
    wt, wst, wout, al = s5_chunk_matrices(*s5)
    ys = s5_scan(u, wt.astype(BF16), wst.astype(BF16), wout.astype(BF16), al, n_batch)
    ys = ys.reshape(n_grp, nc, S5_CHUNK, S5_GROUP).transpose(1, 2, 0, 3).reshape(t, half)
    return even_post(z, ys, d_skip.astype(F32), w_glu.astype(BF16), b_glu.astype(F32), pool_w.astype(BF16),
                     pool_scale.astype(F32), w_out.astype(BF16), h, seq)


def _odd_layer(h, n_batch, layer, g_mix, cos, sin, w_qkv, w_out, q_gain, k_gain, lq1, lk1, lq2, lk2, subln_g):
    q, k, v = attn_qkv(h, g_mix, w_qkv.astype(BF16), cos, sin, q_gain, k_gain)
    lam_init = 0.8 - 0.6 * math.exp(-0.3 * layer)
    o = flash_diff_attention(q, k, v, lq1, lk1, lq2, lk2, subln_g, n_batch, lam_init)
    return proj_residual(o, w_out.astype(BF16), h)


def _moe_layer(h, acc, n_batch, g_ffn, w_router, w_gate, w_up, w_down):
    seq = h.shape[0] // n_batch
    cap = CAPACITY_FACTOR * seq // N_EXPERTS
    aff_t = router(h, g_ffn, w_router, n_batch)
    idx, gate = expert_select(aff_t, cap)
    return moe_ffn(h, acc, g_ffn, idx, gate, w_gate.astype(BF16), w_up.astype(BF16), w_down.astype(BF16), n_batch)


def kernel(x, positions, norm_mix_g, norm_ffn_g, hyb_w_in, hyb_w_out, s5_lam_re, s5_lam_im, s5_log_dt, s5_b_re, s5_b_im, s5_c_re, s5_c_im, s5_d, s5_w_glu, s5_b_glu, pool_w, pool_scale, attn_w_qkv, attn_w_out, attn_q_norm_g, attn_k_norm_g, attn_lam_q1, attn_lam_k1, attn_lam_q2, attn_lam_k2, attn_subln_g, moe_w_router, moe_w_gate, moe_w_up, moe_w_down):
    n_batch, seq, d = x.shape
    depth = norm_mix_g.shape[0]
    h = x.reshape(n_batch * seq, d)
    cos, sin = rope_tables(positions)
    for layer in range(depth):
        if layer % 2 == 0:
            e = layer // 2
            s5 = (s5_lam_re[e], s5_lam_im[e], s5_log_dt[e], s5_b_re[e], s5_b_im[e], s5_c_re[e], s5_c_im[e])
            h, acc = _even_layer(h, n_batch, seq, norm_mix_g[layer], hyb_w_in[e], hyb_w_out[e], s5, s5_d[e],
                                 s5_w_glu[e], s5_b_glu[e], pool_w[e], pool_scale[e])
        else:
            o = layer // 2
            h, acc = _odd_layer(h, n_batch, layer, norm_mix_g[layer], cos, sin, attn_w_qkv[o], attn_w_out[o],
                                attn_q_norm_g[o], attn_k_norm_g[o], attn_lam_q1[o], attn_lam_k1[o],
                                attn_lam_q2[o], attn_lam_k2[o], attn_subln_g[o])
        h = _moe_layer(h, acc, n_batch, norm_ffn_g[layer], moe_w_router[layer], moe_w_gate[layer],
                       moe_w_up[layer], moe_w_down[layer])
    return h.reshape(n_batch, seq, d)
```

```python
import functools
import math

import jax
import jax.numpy as jnp
from jax import lax
from jax.experimental import pallas as pl
from jax.experimental.pallas import tpu as pltpu

F32 = jnp.float32
BF16 = jnp.bfloat16

EPS = 1e-6
LANES = 128
VMEM_LIMIT = 56 * 1024 * 1024

S5_GROUP = 16
S5_CHUNK = 16
POOL_WINDOWS = (2, 4, 8, 16)
POOL_HALO = 8
ATT_HEAD_DIM = 64
ROPE_THETA = 10000.0
N_EXPERTS = 16
CAPACITY_FACTOR = 2


def _row_tile(n, want):
    t = min(n, want)
    assert n % t == 0, (n, t)
    return t


def _params(*sem):
    return pltpu.CompilerParams(dimension_semantics=sem, vmem_limit_bytes=VMEM_LIMIT)


def _rmsnorm_rows(x, g):
    return x * lax.rsqrt(jnp.mean(x * x, axis=-1, keepdims=True) + EPS) * g


def _split_bf16(x):
    hi = x.astype(BF16)
    lo = (x - hi.astype(F32)).astype(BF16)
    return hi, lo


def _norm_proj_kernel(h_ref, g_ref, w_ref, o_ref):
    y = _rmsnorm_rows(h_ref[...], g_ref[...])
    o_ref[...] = jnp.dot(y.astype(BF16), w_ref[...], preferred_element_type=F32).astype(o_ref.dtype)


def norm_proj(h, g, w, out_dtype):
    t, d = h.shape
    n = w.shape[1]
    tm = _row_tile(t, 512)
    return pl.pallas_call(
        _norm_proj_kernel,
        out_shape=jax.ShapeDtypeStruct((t, n), out_dtype),
        grid=(t // tm,),
        in_specs=[pl.BlockSpec((tm, d), lambda i: (i, 0)),
                  pl.BlockSpec((1, d), lambda i: (0, 0)),
                  pl.BlockSpec((d, n), lambda i: (0, 0))],
        out_specs=pl.BlockSpec((tm, n), lambda i: (i, 0)),
        compiler_params=_params("parallel"),
        name="norm_proj",
    )(h, g.reshape(1, d), w)


def _s5_kernel(u_ref, wt_ref, wst_ref, wout_ref, al_ref, y_ref, s_ref, hf_ref, hb_ref, *, n_batch):
    nc_all = u_ref.shape[1]
    nc = nc_all // n_batch
    u = u_ref[0]
    s_ref[...] = jnp.dot(u, wst_ref[0], preferred_element_type=F32)
    al = al_ref[0]
    a_re = al[0:1, :]
    a_im = al[1:2, :]
    is_fwd = lax.broadcasted_iota(jnp.int32, (1, 2 * LANES), 1) % LANES < LANES // 2

    def step(i, carry):
        new = []
        for b in range(n_batch):
            h = carry[b]
            row_f = b * nc + i
            row_b = b * nc + nc - 1 - i
            hf_ref[pl.ds(row_f, 1), :] = h
            hb_ref[pl.ds(row_b, 1), :] = h
            s = jnp.where(is_fwd, s_ref[pl.ds(row_f, 1), :], s_ref[pl.ds(row_b, 1), :])
            h_re = h[:, :LANES]
            h_im = h[:, LANES:]
            n_re = a_re * h_re - a_im * h_im + s[:, :LANES]
            n_im = a_re * h_im + a_im * h_re + s[:, LANES:]
            new.append(jnp.concatenate([n_re, n_im], axis=1))
        return tuple(new)

    zero = jnp.zeros((1, 2 * LANES), F32)
    lax.fori_loop(0, nc, step, tuple(zero for _ in range(n_batch)))
    h_in = jnp.where(is_fwd, hf_ref[...], hb_ref[...])
    y = jnp.dot(u, wt_ref[0], preferred_element_type=F32)
    y = y + jnp.dot(h_in.astype(BF16), wout_ref[0], preferred_element_type=F32)
    y_ref[0] = y.astype(y_ref.dtype)


def s5_scan(u, wt, wst, wout, al, n_batch):
    g, nc_all, w = u.shape
    kern = functools.partial(_s5_kernel, n_batch=n_batch)
    mat = pl.BlockSpec((1, w, w), lambda i: (i, 0, 0))
    return pl.pallas_call(
        kern,
        out_shape=jax.ShapeDtypeStruct((g, nc_all, w), BF16),
        grid=(g,),
        in_specs=[pl.BlockSpec((1, nc_all, w), lambda i: (i, 0, 0)), mat, mat, mat,
                  pl.BlockSpec((1, 2, LANES), lambda i: (i, 0, 0))],
        out_specs=pl.BlockSpec((1, nc_all, w), lambda i: (i, 0, 0)),
        scratch_shapes=[pltpu.VMEM((nc_all, w), F32), pltpu.VMEM((nc_all, w), F32),
                        pltpu.VMEM((nc_all, w), F32)],
        compiler_params=_params("parallel"),
        name="s5_scan",
    )(u, wt, wst, wout, al)


def s5_chunk_matrices(lam_re, lam_im, log_dt, b_re, b_im, c_re, c_im):
    L = S5_CHUNK
    lam = lax.complex(lam_re.astype(F32), lam_im.astype(F32))
    dt = jnp.exp(log_dt.astype(F32))[..., None]
    lam_dt = lam * dt
    a_bar = jnp.exp(lam_dt)
    b = lax.complex(b_re.astype(F32), b_im.astype(F32))
    b_bar = ((a_bar - 1.0) / lam)[..., None] * b
    c = lax.complex(c_re.astype(F32), c_im.astype(F32))
    k = jnp.arange(L + 1, dtype=F32)
    pw = jnp.exp(lam_dt[:, :, None, :] * k[None, None, :, None])
    n_grp, n_state = lam.shape[1], lam.shape[2]
    kern = jnp.einsum('dgop,dgkp,dgpi->dgkoi', c, pw[:, :, :L], b_bar).real
    t_idx = jnp.arange(L)[None, :]
    s_idx = jnp.arange(L)[:, None]
    lag_f = t_idx - s_idx
    kf = jnp.where((lag_f >= 0)[None, :, :, None, None], kern[0][:, jnp.clip(lag_f, 0, L - 1)], 0.0)
    kb = jnp.where((lag_f <= 0)[None, :, :, None, None], kern[1][:, jnp.clip(-lag_f, 0, L - 1)], 0.0)
    wt = (kf + kb).transpose(0, 1, 4, 2, 3).reshape(n_grp, L * S5_GROUP, L * S5_GROUP)
    cf = pw[0][:, L - 1 - jnp.arange(L)][..., None] * b_bar[0][:, None]
    cb = pw[1][:, jnp.arange(L)][..., None] * b_bar[1][:, None]
    def st(x):
        return x.transpose(0, 1, 3, 2).reshape(n_grp, L * S5_GROUP, n_state)
    wst = jnp.concatenate([st(cf.real), st(cb.real), st(cf.imag), st(cb.imag)], axis=-1)
    of = c[0][:, None] * pw[0][:, 1 + jnp.arange(L)][:, :, None, :]
    ob = c[1][:, None] * pw[1][:, L - jnp.arange(L)][:, :, None, :]
    def ot(x):
        return x.transpose(0, 3, 1, 2).reshape(n_grp, n_state, L * S5_GROUP)
    wout = jnp.concatenate([ot(of.real), ot(ob.real), ot(-of.imag), ot(-ob.imag)], axis=1)
    a_l = pw[:, :, L]
    al = jnp.stack([jnp.concatenate([a_l[0].real, a_l[1].real], axis=-1),
                    jnp.concatenate([a_l[0].imag, a_l[1].imag], axis=-1)], axis=1)
    return wt, wst, wout, al


def _gelu_tanh(x):
    return 0.5 * x * (1.0 + jnp.tanh(math.sqrt(2.0 / math.pi) * (x + 0.044715 * (x * x * x))))


def _even_post_kernel(z_ref, zp_ref, zn_ref, ys_ref, d_ref, wglu_ref, bglu_ref, pw_ref, ps_ref, wout_ref,
                      h_ref, o_ref, o2_ref, ext_ref, cat_ref, *, seq):
    tm = z_ref.shape[0]
    half = z_ref.shape[1] // 2
    t0 = (pl.program_id(0) * tm) % seq
    y = d_ref[...] * z_ref[:, :half] + ys_ref[...].astype(F32)
    y = _gelu_tanh(y)
    gate = jnp.dot(y.astype(BF16), wglu_ref[...], preferred_element_type=F32) + bglu_ref[...]
    cat_ref[:, :half] = (y * (1.0 / (1.0 + jnp.exp(-gate)))).astype(BF16)
    v = z_ref[:, half:]
    ext_ref[0:POOL_HALO, :] = jnp.where(t0 > 0, zp_ref[...], 0.0)
    ext_ref[POOL_HALO:POOL_HALO + tm, :] = v
    ext_ref[POOL_HALO + tm:, :] = jnp.where(t0 + tm < seq, zn_ref[...], 0.0)
    pos = t0 + lax.broadcasted_iota(jnp.int32, (tm, 1), 0)
    grp = half // len(POOL_WINDOWS)
    for gi, win in enumerate(POOL_WINDOWS):
        cols = slice(gi * grp, (gi + 1) * grp)
        acc = ext_ref[POOL_HALO - win // 2:POOL_HALO - win // 2 + tm, cols]
        for j in range(1, win):
            off = POOL_HALO - win // 2 + j
            acc = acc + ext_ref[off:off + tm, cols]
        cnt = jnp.minimum(pos + win // 2, seq) - jnp.maximum(pos - win // 2, 0)
        pooled = acc / cnt.astype(F32) - v[:, cols]
        yb = jnp.dot(pooled.astype(BF16), pw_ref[gi], preferred_element_type=F32) * ps_ref[:, cols]
        cat_ref[:, half + gi * grp:half + (gi + 1) * grp] = yb.astype(BF16)
    out = h_ref[...] + jnp.dot(cat_ref[...], wout_ref[...], preferred_element_type=F32)
    o_ref[...] = out
    o2_ref[...] = out


def even_post(z, ys, d_skip, w_glu, b_glu, pool_w, pool_scale, w_out, h, seq):
    t, width = z.shape
    half = width // 2
    d = h.shape[1]
    tm = _row_tile(seq, 512)
    hb = tm // POOL_HALO
    nblk8 = t // POOL_HALO
    kern = functools.partial(_even_post_kernel, seq=seq)
    full = lambda shape: pl.BlockSpec(shape, lambda i: (0,) * len(shape))
    return pl.pallas_call(
        kern,
        out_shape=(jax.ShapeDtypeStruct((t, d), F32), jax.ShapeDtypeStruct((t, d), F32)),
        grid=(t // tm,),
        in_specs=[pl.BlockSpec((tm, width), lambda i: (i, 0)),
                  pl.BlockSpec((POOL_HALO, half), lambda i: (jnp.maximum(i * hb - 1, 0), 1)),
                  pl.BlockSpec((POOL_HALO, half), lambda i: (jnp.minimum((i + 1) * hb, nblk8 - 1), 1)),
                  pl.BlockSpec((tm, half), lambda i: (i, 0)),
                  full((1, half)), full((half, half)), full((1, half)),
                  full(pool_w.shape), full((1, half)), full((width, d)),
                  pl.BlockSpec((tm, d), lambda i: (i, 0))],
        out_specs=(pl.BlockSpec((tm, d), lambda i: (i, 0)), pl.BlockSpec((tm, d), lambda i: (i, 0))),
        scratch_shapes=[pltpu.VMEM((tm + 2 * POOL_HALO, half), F32), pltpu.VMEM((tm, width), BF16)],
        compiler_params=_params("parallel"),
        name="even_post",
    )(z, z, z, ys, d_skip.reshape(1, half), w_glu, b_glu.reshape(1, half), pool_w,
      pool_scale.reshape(1, half), w_out, h)


def _rope_kernel(pos_ref, inv_ref, sign_ref, cos_ref, sin_ref):
    ang = pos_ref[...].astype(F32) * inv_ref[...]
    cos_ref[...] = jnp.cos(ang)
    sin_ref[...] = jnp.sin(ang) * sign_ref[...]


def rope_tables(positions):
    t = positions.size
    half = ATT_HEAD_DIM // 2
    inv = ROPE_THETA ** (-jnp.arange(0, ATT_HEAD_DIM, 2, dtype=F32) / ATT_HEAD_DIM)
    reps = LANES // half
    inv_pat = jnp.tile(inv, reps).reshape(1, LANES)
    sign = jnp.tile(jnp.concatenate([-jnp.ones((half,), F32), jnp.ones((half,), F32)]), reps // 2).reshape(1, LANES)
    pos = jnp.broadcast_to(positions.reshape(t, 1), (t, LANES))
    tm = _row_tile(t, 1024)
    return pl.pallas_call(
        _rope_kernel,
        out_shape=(jax.ShapeDtypeStruct((t, LANES), F32), jax.ShapeDtypeStruct((t, LANES), F32)),
        grid=(t // tm,),
        in_specs=[pl.BlockSpec((tm, LANES), lambda i: (i, 0)),
                  pl.BlockSpec((1, LANES), lambda i: (0, 0)),
                  pl.BlockSpec((1, LANES), lambda i: (0, 0))],
        out_specs=(pl.BlockSpec((tm, LANES), lambda i: (i, 0)), pl.BlockSpec((tm, LANES), lambda i: (i, 0))),
        compiler_params=_params("parallel"),
        name="rope_tables",
    )(pos, inv_pat, sign)


ATT_KV_CHUNK = 512


def _qkv_kernel(h_ref, g_ref, wqk_ref, wvt_ref, cos_ref, sin_ref, qg_ref, kg_ref, seg_ref, q_ref, k_ref, vt_ref):
    d = h_ref.shape[1]
    y = _rmsnorm_rows(h_ref[...], g_ref[...]).astype(BF16)
    cos = cos_ref[...]
    sin = sin_ref[...]
    seg = seg_ref[...]
    first_half = lax.broadcasted_iota(jnp.int32, (1, LANES), 1) % ATT_HEAD_DIM < ATT_HEAD_DIM // 2
    q_scale = ATT_HEAD_DIM ** -0.5 * math.log2(math.e)
    for which, (o_ref, gain_ref, scale) in enumerate(((q_ref, qg_ref, q_scale), (k_ref, kg_ref, 1.0))):
        x_all = jnp.dot(y, wqk_ref[:, which * d:(which + 1) * d], preferred_element_type=F32)
        for j in range(d // LANES):
            x = x_all[:, j * LANES:(j + 1) * LANES]
            hi, lo = _split_bf16(x * x)
            ss = jnp.dot(hi, seg, preferred_element_type=F32) + jnp.dot(lo, seg, preferred_element_type=F32)
            xn = x * lax.rsqrt(ss * (1.0 / ATT_HEAD_DIM) + EPS) * gain_ref[...]
            swapped = jnp.where(first_half, pltpu.roll(xn, LANES - ATT_HEAD_DIM // 2, 1),
                                pltpu.roll(xn, ATT_HEAD_DIM // 2, 1))
            o_ref[:, j * LANES:(j + 1) * LANES] = ((xn * cos + swapped * sin) * scale).astype(BF16)
    vt = lax.dot_general(wvt_ref[...], y, (((1,), (1,)), ((), ())), preferred_element_type=F32)
    vt_ref[0] = vt.reshape(vt_ref.shape[1:]).astype(BF16)


def attn_qkv(h, g, w_qkv, cos, sin, q_gain, k_gain):
    t, d = h.shape
    tm = _row_tile(t, ATT_KV_CHUNK)
    heads = d // LANES
    reps = LANES // ATT_HEAD_DIM
    lane = jnp.arange(LANES)
    seg = (lane[:, None] // ATT_HEAD_DIM == lane[None, :] // ATT_HEAD_DIM).astype(BF16)
    w_qk = w_qkv[:, :2 * d].astype(BF16)
    w_vt = w_qkv[:, 2 * d:].T.astype(BF16)
    row = lambda n: pl.BlockSpec((tm, n), lambda i: (i, 0))
    one = lambda n: pl.BlockSpec((1, n), lambda i: (0, 0))
    out = jax.ShapeDtypeStruct((t, d), BF16)
    return pl.pallas_call(
        _qkv_kernel,
        out_shape=(out, out, jax.ShapeDtypeStruct((t // tm, heads, LANES, tm), BF16)),
        grid=(t // tm,),
        in_specs=[row(d), one(d), pl.BlockSpec(w_qk.shape, lambda i: (0, 0)), pl.BlockSpec(w_vt.shape, lambda i: (0, 0)),
                  row(LANES), row(LANES), one(LANES), one(LANES), pl.BlockSpec((LANES, LANES), lambda i: (0, 0))],
        out_specs=(row(d), row(d), pl.BlockSpec((1, heads, LANES, tm), lambda i: (i, 0, 0, 0))),
        compiler_params=_params("parallel"),
        name="attn_qkv",
    )(h, g.reshape(1, d), w_qk, w_vt, cos, sin, jnp.tile(q_gain.astype(F32), reps).reshape(1, LANES),
      jnp.tile(k_gain.astype(F32), reps).reshape(1, LANES), seg)


def _flash_kernel(q_ref, k_ref, vt_ref, lq1_ref, lk1_ref, lq2_ref, lk2_ref, sg_ref, o_ref, acc_ref, sta_ref, stb_ref,
                  *, lam_init):
    tq = q_ref.shape[0]
    nk, _, _, tk = vt_ref.shape
    q = q_ref[...]
    first = lax.broadcasted_iota(jnp.int32, (1, LANES), 1) < ATT_HEAD_DIM
    zero = jnp.zeros_like(q)
    qm = jnp.concatenate([jnp.where(first, q, zero), jnp.where(first, zero, q)], axis=0)
    acc_ref[...] = jnp.zeros(acc_ref.shape, F32)
    nt = (((1,), (1,)), ((), ()))

    def scores(i, st_ref):
        kc = k_ref[pl.ds(pl.multiple_of(i * tk, tk), tk), :]
        st_ref[...] = lax.dot_general(kc, qm, nt, preferred_element_type=F32)

    def absorb(i, st_ref, m_prev, l8):
        st = st_ref[...]
        m_new = jnp.maximum(m_prev, jnp.max(st, axis=0, keepdims=True))
        alpha = jnp.exp2(m_prev - m_new)
        p = jnp.exp2(st - m_new)
        l8 = alpha * l8 + jnp.sum(p.reshape(tk // 8, 8, 2 * tq), axis=0)
        acc_ref[...] = alpha * acc_ref[...] + jnp.dot(vt_ref[i, 0], p.astype(BF16), preferred_element_type=F32)
        return m_new, l8

    def body(j, carry):
        m, l8 = carry
        scores(2 * j + 1, stb_ref)
        m, l8 = absorb(2 * j, sta_ref, m, l8)
        scores(jnp.minimum(2 * j + 2, nk - 1), sta_ref)
        return absorb(2 * j + 1, stb_ref, m, l8)

    assert nk % 2 == 0
    scores(0, sta_ref)
    m0 = jnp.full((1, 2 * tq), -jnp.inf, F32)
    _, l8 = lax.fori_loop(0, nk // 2, body, (m0, jnp.zeros((8, 2 * tq), F32)))
    lam = (jnp.exp(jnp.sum(lq1_ref[...] * lk1_ref[...], axis=-1, keepdims=True))
           - jnp.exp(jnp.sum(lq2_ref[...] * lk2_ref[...], axis=-1, keepdims=True)) + lam_init)
    o = acc_ref[...] / jnp.sum(l8, axis=0, keepdims=True)
    o = o[:, :tq] - lam * o[:, tq:]
    o = o * lax.rsqrt(jnp.mean(o * o, axis=0, keepdims=True) + EPS) * sg_ref[...]
    o_ref[...] = (o * (1.0 - lam_init)).T.astype(o_ref.dtype)


def flash_diff_attention(q, k, vt, lam_q1, lam_k1, lam_q2, lam_k2, subln_g, n_batch, lam_init):
    t, width = q.shape
    seq = t // n_batch
    heads = width // LANES
    tk = vt.shape[-1]
    tq = _row_tile(seq, 256)
    nq, nk = seq // tq, seq // tk
    kern = functools.partial(_flash_kernel, lam_init=lam_init)
    vec = lambda n: pl.BlockSpec((1, n), lambda b, h, qi: (0, 0))
    lam_vec = lambda x: x.astype(F32).reshape(1, ATT_HEAD_DIM)
    return pl.pallas_call(
        kern,
        out_shape=jax.ShapeDtypeStruct((t, width), BF16),
        grid=(n_batch, heads, nq),
        in_specs=[pl.BlockSpec((tq, LANES), lambda b, h, qi: (b * nq + qi, h)),
                  pl.BlockSpec((seq, LANES), lambda b, h, qi: (b, h)),
                  pl.BlockSpec((nk, 1, LANES, tk), lambda b, h, qi: (b, h, 0, 0)),
                  vec(ATT_HEAD_DIM), vec(ATT_HEAD_DIM), vec(ATT_HEAD_DIM), vec(ATT_HEAD_DIM),
                  pl.BlockSpec((LANES, 1), lambda b, h, qi: (0, 0))],
        out_specs=pl.BlockSpec((tq, LANES), lambda b, h, qi: (b * nq + qi, h)),
        scratch_shapes=[pltpu.VMEM((LANES, 2 * tq), F32), pltpu.VMEM((tk, 2 * tq), F32),
                        pltpu.VMEM((tk, 2 * tq), F32)],
        compiler_params=_params("parallel", "parallel", "arbitrary"),
        name="flash_diff_attention",
    )(q, k, vt, lam_vec(lam_q1), lam_vec(lam_k1), lam_vec(lam_q2), lam_vec(lam_k2),
      subln_g.astype(F32).reshape(LANES, 1))


def _proj_residual_kernel(x_ref, w_ref, h_ref, o_ref, o2_ref):
    out = h_ref[...] + jnp.dot(x_ref[...], w_ref[...], preferred_element_type=F32)
    o_ref[...] = out
    o2_ref[...] = out


def proj_residual(x, w, h):
    t, d = h.shape
    kdim = x.shape[1]
    tm = _row_tile(t, 512)
    row = lambda n: pl.BlockSpec((tm, n), lambda i: (i, 0))
    out = jax.ShapeDtypeStruct((t, d), F32)
    return pl.pallas_call(
        _proj_residual_kernel,
        out_shape=(out, out),
        grid=(t // tm,),
        in_specs=[row(kdim), pl.BlockSpec((kdim, d), lambda i: (0, 0)), row(d)],
        out_specs=(row(d), row(d)),
        compiler_params=_params("parallel"),
        name="proj_residual",
    )(x, w, h)


def _router_kernel(h_ref, g_ref, wt_hi_ref, wt_lo_ref, aff_ref):
    y = _rmsnorm_rows(h_ref[...], g_ref[...])
    y_hi, y_lo = _split_bf16(y)
    nt = (((1,), (1,)), ((), ()))
    logits = (lax.dot_general(wt_hi_ref[...], y_hi, nt, preferred_element_type=F32)
              + lax.dot_general(wt_hi_ref[...], y_lo, nt, preferred_element_type=F32)
              + lax.dot_general(wt_lo_ref[...], y_hi, nt, preferred_element_type=F32))
    e = jnp.exp(logits - jnp.max(logits, axis=0, keepdims=True))
    aff_ref[0] = e / jnp.sum(e, axis=0, keepdims=True)


def router(h, g, w_router, n_batch):
    t, d = h.shape
    seq = t // n_batch
    n_exp = w_router.shape[1]
    tm = _row_tile(seq, 512)
    ns = seq // tm
    wt = w_router.astype(F32).T
    wt_hi, wt_lo = _split_bf16(wt)
    return pl.pallas_call(
        _router_kernel,
        out_shape=jax.ShapeDtypeStruct((n_batch, n_exp, seq), F32),
        grid=(n_batch, ns),
        in_specs=[pl.BlockSpec((tm, d), lambda b, i: (b * ns + i, 0)),
                  pl.BlockSpec((1, d), lambda b, i: (0, 0)),
                  pl.BlockSpec((n_exp, d), lambda b, i: (0, 0)),
                  pl.BlockSpec((n_exp, d), lambda b, i: (0, 0))],
        out_specs=pl.BlockSpec((1, n_exp, tm), lambda b, i: (b, 0, i)),
        compiler_params=_params("parallel", "parallel"),
        name="router",
    )(h, g.reshape(1, d), wt_hi, wt_lo)


def _select_kernel(aff_ref, affw_ref, idx_ref, gate_ref, *, cap):
    rows_all = aff_ref.shape[1]
    n_exp = idx_ref.shape[1]
    r = rows_all // n_exp
    aff = aff_ref[0]
    bits_wide = pltpu.bitcast(affw_ref[0], jnp.int32)

    def count(mask):
        return jnp.sum(jnp.where(mask, 1.0, 0.0), axis=1, keepdims=True)

    def bit_step(i, thr):
        cand = thr | jnp.left_shift(jnp.int32(1), 30 - i)
        return jnp.where(count(bits_wide >= cand) >= cap, cand, thr)

    thr = lax.fori_loop(0, 31, bit_step, jnp.zeros((n_exp, 1), jnp.int32))
    n_tie_take = cap - count(bits_wide > thr)

    tri_incl = (lax.broadcasted_iota(jnp.int32, (LANES, LANES), 0)
                <= lax.broadcasted_iota(jnp.int32, (LANES, LANES), 1)).astype(BF16)
    ones_mat = jnp.ones((LANES, LANES), BF16)
    low_strict = (lax.broadcasted_iota(jnp.int32, (r, r), 1)
                  < lax.broadcasted_iota(jnp.int32, (r, r), 0)).astype(BF16)

    def prefix(mask2d):
        m = jnp.where(mask2d, 1.0, 0.0).astype(BF16)
        lane_incl = jnp.dot(m, tri_incl, preferred_element_type=F32)
        row_tot = jnp.dot(m, ones_mat, preferred_element_type=F32)
        row_off = jnp.dot(low_strict, row_tot.astype(BF16), preferred_element_type=F32)
        return lane_incl, row_tot, row_off

    slot = lax.broadcasted_iota(jnp.int32, (1, cap), 1).astype(F32)
    row_id = lax.broadcasted_iota(jnp.int32, (r, cap), 0).astype(F32)
    lane_id = lax.broadcasted_iota(jnp.int32, (LANES, cap), 0).astype(F32)
    tn = (((0,), (0,)), ((), ()))
    for e in range(n_exp):
        a_e = aff[e * r:(e + 1) * r, :]
        bits = pltpu.bitcast(a_e, jnp.int32)
        thr_e = thr[e:e + 1, :]
        tie_e = bits == thr_e
        t_incl, _, t_off = prefix(tie_e)
        tie_rank = t_off + t_incl - 1.0
        take = jnp.where(tie_rank < n_tie_take[e:e + 1, :], 1.0, 0.0) * jnp.where(tie_e, 1.0, 0.0)
        sel = jnp.where(bits > thr_e, 1.0, take) > 0.5
        lane_incl, row_tot, row_off = prefix(sel)
        off_b = jnp.tile(row_off, (1, cap // LANES))
        cum_b = jnp.tile(row_off + row_tot, (1, cap // LANES))
        row_of = jnp.sum(jnp.where(cum_b <= slot, 1.0, 0.0), axis=0, keepdims=True)
        off_of = jnp.max(jnp.where(off_b <= slot, off_b, 0.0), axis=0, keepdims=True)
        local = slot - off_of
        onehot = jnp.where(row_id == row_of, 1.0, 0.0).astype(BF16)
        incl_t = lax.dot_general(lane_incl.astype(BF16), onehot, tn, preferred_element_type=F32)
        lane_of = jnp.sum(jnp.where(incl_t <= local, 1.0, 0.0), axis=0, keepdims=True)
        idx_ref[0, e:e + 1, :] = (row_of * LANES + lane_of).astype(jnp.int32)
        a_hi = a_e.astype(BF16)
        a_mid = (a_e - a_hi.astype(F32)).astype(BF16)
        a_lo = (a_e - a_hi.astype(F32) - a_mid.astype(F32)).astype(BF16)
        a_t = (lax.dot_general(a_hi, onehot, tn, preferred_element_type=F32)
               + lax.dot_general(a_mid, onehot, tn, preferred_element_type=F32)
               + lax.dot_general(a_lo, onehot, tn, preferred_element_type=F32))
        gate_ref[0, e:e + 1, :] = jnp.sum(jnp.where(lane_id == lane_of, a_t, 0.0), axis=0, keepdims=True)


def expert_select(aff_t, cap):
    n_batch, n_exp, seq = aff_t.shape
    rows = n_exp * seq // LANES
    kern = functools.partial(_select_kernel, cap=cap)
    return pl.pallas_call(
        kern,
        out_shape=(jax.ShapeDtypeStruct((n_batch, n_exp, cap), jnp.int32),
                   jax.ShapeDtypeStruct((n_batch, n_exp, cap), F32)),
        grid=(n_batch,),
        in_specs=[pl.BlockSpec((1, rows, LANES), lambda b: (b, 0, 0)),
                  pl.BlockSpec((1, n_exp, seq), lambda b: (b, 0, 0))],
        out_specs=(pl.BlockSpec((1, n_exp, cap), lambda b: (b, 0, 0)),
                   pl.BlockSpec((1, n_exp, cap), lambda b: (b, 0, 0))),
        compiler_params=_params("parallel"),
        name="expert_select",
    )(aff_t.reshape(n_batch, rows, LANES), aff_t)


DMA_UNROLL = 8


def _cast_kernel(x_ref, o_ref):
    o_ref[...] = x_ref[...].astype(o_ref.dtype)


def cast_layer_bf16(w, layer):
    _, n, r, c = w.shape
    return pl.pallas_call(
        _cast_kernel,
        out_shape=jax.ShapeDtypeStruct((n, r, c), BF16),
        grid=(n,),
        in_specs=[pl.BlockSpec((None, 1, r, c), lambda i: (layer, i, 0, 0))],
        out_specs=pl.BlockSpec((1, r, c), lambda i: (i, 0, 0)),
        compiler_params=_params("parallel"),
        name="cast_bf16",
    )(w)


def _moe_kernel(idx_ref, gate_ref, g_ref, wg_ref, wu_ref, wd_ref, hin_ref, acc_in_ref, acc_ref,
                x_buf, o_buf, sem, *, seq, cap):
    del acc_in_ref
    e, b, ti = pl.program_id(0), pl.program_id(1), pl.program_id(2)
    tm = x_buf.shape[0]
    n_exp = pl.num_programs(0)
    base = (b * n_exp + e) * cap + ti * tm
    row0 = b * seq

    def rows(j):
        return pl.ds(row0 + idx_ref[base + j], 1)

    def start_gather(jj, c):
        for u in range(DMA_UNROLL):
            j = jj * DMA_UNROLL + u
            pltpu.make_async_copy(hin_ref.at[rows(j)], x_buf.at[pl.ds(j, 1)], sem.at[0]).start()
            pltpu.make_async_copy(acc_ref.at[rows(j)], o_buf.at[pl.ds(j, 1)], sem.at[1]).start()
        return c

    lax.fori_loop(0, tm // DMA_UNROLL, start_gather, 0)
    pltpu.make_async_copy(hin_ref.at[pl.ds(0, tm)], x_buf, sem.at[0]).wait()
    pltpu.make_async_copy(acc_ref.at[pl.ds(0, tm)], o_buf, sem.at[1]).wait()

    x = _rmsnorm_rows(x_buf[...], g_ref[...]).astype(BF16)
    a = jnp.dot(x, wg_ref[0], preferred_element_type=F32)
    u = jnp.dot(x, wu_ref[0], preferred_element_type=F32)
    hmid = (a * (1.0 / (1.0 + jnp.exp(-a))) * u).astype(BF16)
    y = jnp.dot(hmid, wd_ref[0], preferred_element_type=F32)
    o_buf[...] = o_buf[...] + y * gate_ref[...]

    def start_scatter(jj, c):
        for u in range(DMA_UNROLL):
            j = jj * DMA_UNROLL + u
            pltpu.make_async_copy(o_buf.at[pl.ds(j, 1)], acc_ref.at[rows(j)], sem.at[2]).start()
        return c

    lax.fori_loop(0, tm // DMA_UNROLL, start_scatter, 0)
    pltpu.make_async_copy(o_buf, acc_ref.at[pl.ds(0, tm)], sem.at[2]).wait()


def moe_ffn(h, acc, g, idx, gate, w_gate, w_up, w_down, n_batch):
    t, d = h.shape
    seq = t // n_batch
    n_exp, _, ff = w_gate.shape
    cap = idx.shape[-1]
    tm = _row_tile(cap, 512)
    nt = cap // tm
    kern = functools.partial(_moe_kernel, seq=seq, cap=cap)
    grid_spec = pltpu.PrefetchScalarGridSpec(
        num_scalar_prefetch=1,
        grid=(n_exp, n_batch, nt),
        in_specs=[pl.BlockSpec((tm, 1), lambda e, b, i, idx: ((b * n_exp + e) * nt + i, 0)),
                  pl.BlockSpec((1, d), lambda e, b, i, idx: (0, 0)),
                  pl.BlockSpec((1, d, ff), lambda e, b, i, idx: (e, 0, 0)),
                  pl.BlockSpec((1, d, ff), lambda e, b, i, idx: (e, 0, 0)),
                  pl.BlockSpec((1, ff, d), lambda e, b, i, idx: (e, 0, 0)),
                  pl.BlockSpec(memory_space=pl.ANY),
                  pl.BlockSpec(memory_space=pl.ANY)],
        out_specs=pl.BlockSpec(memory_space=pl.ANY),
        scratch_shapes=[pltpu.VMEM((tm, d), F32), pltpu.VMEM((tm, d), F32), pltpu.SemaphoreType.DMA((3,))],
    )
    return pl.pallas_call(
        kern,
        out_shape=jax.ShapeDtypeStruct((t, d), F32),
        grid_spec=grid_spec,
        input_output_aliases={7: 0},
        compiler_params=_params("arbitrary", "arbitrary", "arbitrary"),
        name="moe_ffn",
    )(idx.reshape(-1), gate.reshape(-1, 1), g.reshape(1, d), w_gate, w_up, w_down, h, acc)


def _even_layer(h, n_batch, seq, g_mix, w_in, w_out, s5, d_skip, w_glu, b_glu, pool_w, pool_scale):
    t = h.shape[0]
    z = norm_proj(h, g_mix, w_in.astype(BF16), F32)
    half = z.shape[1] // 2
    n_grp = half // S5_GROUP
    nc = t // S5_CHUNK
    u = z[:, :half].astype(BF16).reshape(nc, S5_CHUNK, n_grp, S5_GROUP)
    u = u.transpose(2, 0, 1, 3).reshape(n_grp, nc, S5_CHUNK * S5_GROUP)
    wt, wst, wout, al = s5_chunk_matrices(*s5)
    ys = s5_scan(u, wt.astype(BF16), wst.astype(BF16), wout.astype(BF16), al, n_batch)
    ys = ys.reshape(n_grp, nc, S5_CHUNK, S5_GROUP).transpose(1, 2, 0, 3).reshape(t, half)
    return even_post(z, ys, d_skip.astype(F32), w_glu.astype(BF16), b_glu.astype(F32), pool_w.astype(BF16),
                     pool_scale.astype(F32), w_out.astype(BF16), h, seq)


def _odd_layer(h, n_batch, layer, g_mix, cos, sin, w_qkv, w_out, q_gain, k_gain, lq1, lk1, lq2, lk2, subln_g):
    q, k, vt = attn_qkv(h, g_mix, w_qkv, cos, sin, q_gain, k_gain)
    lam_init = 0.8 - 0.6 * math.exp(-0.3 * layer)
    o = flash_diff_attention(q, k, vt, lq1, lk1, lq2, lk2, subln_g, n_batch, lam_init)
    return proj_residual(o, w_out.astype(BF16), h)


def _moe_layer(h, acc, n_batch, layer, g_ffn, w_router, w_gate_all, w_up_all, w_down_all):
    seq = h.shape[0] // n_batch
    cap = CAPACITY_FACTOR * seq // N_EXPERTS
    aff_t = router(h, g_ffn, w_router, n_batch)
    idx, gate = expert_select(aff_t, cap)
    return moe_ffn(h, acc, g_ffn, idx, gate, cast_layer_bf16(w_gate_all, layer), cast_layer_bf16(w_up_all, layer),
                   cast_layer_bf16(w_down_all, layer), n_batch)


def kernel(x, positions, norm_mix_g, norm_ffn_g, hyb_w_in, hyb_w_out, s5_lam_re, s5_lam_im, s5_log_dt, s5_b_re, s5_b_im, s5_c_re, s5_c_im, s5_d, s5_w_glu, s5_b_glu, pool_w, pool_scale, attn_w_qkv, attn_w_out, attn_q_norm_g, attn_k_norm_g, attn_lam_q1, attn_lam_k1, attn_lam_q2, attn_lam_k2, attn_subln_g, moe_w_router, moe_w_gate, moe_w_up, moe_w_down):
    n_batch, seq, d = x.shape
    depth = norm_mix_g.shape[0]
    h = x.reshape(n_batch * seq, d)
    cos, sin = rope_tables(positions)
    for layer in range(depth):
        if layer % 2 == 0:
            e = layer // 2
            s5 = (s5_lam_re[e], s5_lam_im[e], s5_log_dt[e], s5_b_re[e], s5_b_im[e], s5_c_re[e], s5_c_im[e])
            h, acc = _even_layer(h, n_batch, seq, norm_mix_g[layer], hyb_w_in[e], hyb_w_out[e], s5, s5_d[e],
                                 s5_w_glu[e], s5_b_glu[e], pool_w[e], pool_scale[e])
        else:
            o = layer // 2
            h, acc = _odd_layer(h, n_batch, layer, norm_mix_g[layer], cos, sin, attn_w_qkv[o], attn_w_out[o],
                                attn_q_norm_g[o], attn_k_norm_g[o], attn_lam_q1[o], attn_lam_k1[o],
                                attn_lam_q2[o], attn_lam_k2[o], attn_subln_g[o])
        h = _moe_layer(h, acc, n_batch, layer, norm_ffn_g[layer], moe_w_router[layer], moe_w_gate,
                       moe_w_up, moe_w_down)
    return h.reshape(n_batch, seq, d)
```

```python
import functools
import math

import jax
import jax.numpy as jnp
from jax import lax
from jax.experimental import pallas as pl
from jax.experimental.pallas import tpu as pltpu

F32 = jnp.float32
BF16 = jnp.bfloat16

EPS = 1e-6
LANES = 128
VMEM_LIMIT = 56 * 1024 * 1024

S5_GROUP = 16
S5_CHUNK = 16
POOL_WINDOWS = (2, 4, 8, 16)
POOL_HALO = 8
ATT_HEAD_DIM = 64
ROPE_THETA = 10000.0
N_EXPERTS = 16
CAPACITY_FACTOR = 2


def _row_tile(n, want):
    t = min(n, want)
    assert n % t == 0, (n, t)
    return t


def _params(*sem):
    return pltpu.CompilerParams(dimension_semantics=sem, vmem_limit_bytes=VMEM_LIMIT)


def _rmsnorm_rows(x, g):
    return x * lax.rsqrt(jnp.mean(x * x, axis=-1, keepdims=True) + EPS) * g


def _split_bf16(x):
    hi = x.astype(BF16)
    lo = (x - hi.astype(F32)).astype(BF16)
    return hi, lo


def _norm_proj_kernel(h_ref, g_ref, w_ref, o_ref):
    y = _rmsnorm_rows(h_ref[...], g_ref[...])
    o_ref[...] = jnp.dot(y.astype(BF16), w_ref[...], preferred_element_type=F32).astype(o_ref.dtype)


def norm_proj(h, g, w, out_dtype):
    t, d = h.shape
    n = w.shape[1]
    tm = _row_tile(t, 512)
    return pl.pallas_call(
        _norm_proj_kernel,
        out_shape=jax.ShapeDtypeStruct((t, n), out_dtype),
        grid=(t // tm,),
        in_specs=[pl.BlockSpec((tm, d), lambda i: (i, 0)),
                  pl.BlockSpec((1, d), lambda i: (0, 0)),
                  pl.BlockSpec((d, n), lambda i: (0, 0))],
        out_specs=pl.BlockSpec((tm, n), lambda i: (i, 0)),
        compiler_params=_params("parallel"),
        name="norm_proj",
    )(h, g.reshape(1, d), w)


def _s5_kernel(u_ref, wt_ref, wst_ref, wout_ref, al_ref, y_ref, s_ref, hf_ref, hb_ref, *, n_batch):
    nc_all = u_ref.shape[1]
    nc = nc_all // n_batch
    u = u_ref[0]
    s_ref[...] = jnp.dot(u, wst_ref[0], preferred_element_type=F32)
    al = al_ref[0]
    a_re = al[0:1, :]
    a_im = al[1:2, :]
    is_fwd = lax.broadcasted_iota(jnp.int32, (1, 2 * LANES), 1) % LANES < LANES // 2

    def step(i, carry):
        new = []
        for b in range(n_batch):
            h = carry[b]
            row_f = b * nc + i
            row_b = b * nc + nc - 1 - i
            hf_ref[pl.ds(row_f, 1), :] = h
            hb_ref[pl.ds(row_b, 1), :] = h
            s = jnp.where(is_fwd, s_ref[pl.ds(row_f, 1), :], s_ref[pl.ds(row_b, 1), :])
            h_re = h[:, :LANES]
            h_im = h[:, LANES:]
            n_re = a_re * h_re - a_im * h_im + s[:, :LANES]
            n_im = a_re * h_im + a_im * h_re + s[:, LANES:]
            new.append(jnp.concatenate([n_re, n_im], axis=1))
        return tuple(new)

    zero = jnp.zeros((1, 2 * LANES), F32)
    lax.fori_loop(0, nc, step, tuple(zero for _ in range(n_batch)))
    h_in = jnp.where(is_fwd, hf_ref[...], hb_ref[...])
    y = jnp.dot(u, wt_ref[0], preferred_element_type=F32)
    y = y + jnp.dot(h_in.astype(BF16), wout_ref[0], preferred_element_type=F32)
    y_ref[0] = y.astype(y_ref.dtype)


def s5_scan(u, wt, wst, wout, al, n_batch):
    g, nc_all, w = u.shape
    kern = functools.partial(_s5_kernel, n_batch=n_batch)
    mat = pl.BlockSpec((1, w, w), lambda i: (i, 0, 0))
    return pl.pallas_call(
        kern,
        out_shape=jax.ShapeDtypeStruct((g, nc_all, w), BF16),
        grid=(g,),
        in_specs=[pl.BlockSpec((1, nc_all, w), lambda i: (i, 0, 0)), mat, mat, mat,
                  pl.BlockSpec((1, 2, LANES), lambda i: (i, 0, 0))],
        out_specs=pl.BlockSpec((1, nc_all, w), lambda i: (i, 0, 0)),
        scratch_shapes=[pltpu.VMEM((nc_all, w), F32), pltpu.VMEM((nc_all, w), F32),
                        pltpu.VMEM((nc_all, w), F32)],
        compiler_params=_params("parallel"),
        name="s5_scan",
    )(u, wt, wst, wout, al)


def s5_chunk_matrices(lam_re, lam_im, log_dt, b_re, b_im, c_re, c_im):
    L = S5_CHUNK
    lam = lax.complex(lam_re.astype(F32), lam_im.astype(F32))
    dt = jnp.exp(log_dt.astype(F32))[..., None]
    lam_dt = lam * dt
    a_bar = jnp.exp(lam_dt)
    b = lax.complex(b_re.astype(F32), b_im.astype(F32))
    b_bar = ((a_bar - 1.0) / lam)[..., None] * b
    c = lax.complex(c_re.astype(F32), c_im.astype(F32))
    k = jnp.arange(L + 1, dtype=F32)
    pw = jnp.exp(lam_dt[:, :, None, :] * k[None, None, :, None])
    n_grp, n_state = lam.shape[1], lam.shape[2]
    kern = jnp.einsum('dgop,dgkp,dgpi->dgkoi', c, pw[:, :, :L], b_bar).real
    t_idx = jnp.arange(L)[None, :]
    s_idx = jnp.arange(L)[:, None]
    lag_f = t_idx - s_idx
    kf = jnp.where((lag_f >= 0)[None, :, :, None, None], kern[0][:, jnp.clip(lag_f, 0, L - 1)], 0.0)
    kb = jnp.where((lag_f <= 0)[None, :, :, None, None], kern[1][:, jnp.clip(-lag_f, 0, L - 1)], 0.0)
    wt = (kf + kb).transpose(0, 1, 4, 2, 3).reshape(n_grp, L * S5_GROUP, L * S5_GROUP)
    cf = pw[0][:, L - 1 - jnp.arange(L)][..., None] * b_bar[0][:, None]
    cb = pw[1][:, jnp.arange(L)][..., None] * b_bar[1][:, None]
    def st(x):
        return x.transpose(0, 1, 3, 2).reshape(n_grp, L * S5_GROUP, n_state)
    wst = jnp.concatenate([st(cf.real), st(cb.real), st(cf.imag), st(cb.imag)], axis=-1)
    of = c[0][:, None] * pw[0][:, 1 + jnp.arange(L)][:, :, None, :]
    ob = c[1][:, None] * pw[1][:, L - jnp.arange(L)][:, :, None, :]
    def ot(x):
        return x.transpose(0, 3, 1, 2).reshape(n_grp, n_state, L * S5_GROUP)
    wout = jnp.concatenate([ot(of.real), ot(ob.real), ot(-of.imag), ot(-ob.imag)], axis=1)
    a_l = pw[:, :, L]
    al = jnp.stack([jnp.concatenate([a_l[0].real, a_l[1].real], axis=-1),
                    jnp.concatenate([a_l[0].imag, a_l[1].imag], axis=-1)], axis=1)
    return wt, wst, wout, al


def _gelu_tanh(x):
    return 0.5 * x * (1.0 + jnp.tanh(math.sqrt(2.0 / math.pi) * (x + 0.044715 * (x * x * x))))


def _even_post_kernel(z_ref, zp_ref, zn_ref, ys_ref, d_ref, wglu_ref, bglu_ref, pw_ref, ps_ref, wout_ref,
                      h_ref, o_ref, o2_ref, ext_ref, cat_ref, *, seq):
    tm = z_ref.shape[0]
    half = z_ref.shape[1] // 2
    t0 = (pl.program_id(0) * tm) % seq
    y = d_ref[...] * z_ref[:, :half] + ys_ref[...].astype(F32)
    y = _gelu_tanh(y)
    gate = jnp.dot(y.astype(BF16), wglu_ref[...], preferred_element_type=F32) + bglu_ref[...]
    cat_ref[:, :half] = (y * (1.0 / (1.0 + jnp.exp(-gate)))).astype(BF16)
    v = z_ref[:, half:]
    ext_ref[0:POOL_HALO, :] = jnp.where(t0 > 0, zp_ref[...], 0.0)
    ext_ref[POOL_HALO:POOL_HALO + tm, :] = v
    ext_ref[POOL_HALO + tm:, :] = jnp.where(t0 + tm < seq, zn_ref[...], 0.0)
    pos = t0 + lax.broadcasted_iota(jnp.int32, (tm, 1), 0)
    grp = half // len(POOL_WINDOWS)
    for gi, win in enumerate(POOL_WINDOWS):
        cols = slice(gi * grp, (gi + 1) * grp)
        acc = ext_ref[POOL_HALO - win // 2:POOL_HALO - win // 2 + tm, cols]
        for j in range(1, win):
            off = POOL_HALO - win // 2 + j
            acc = acc + ext_ref[off:off + tm, cols]
        cnt = jnp.minimum(pos + win // 2, seq) - jnp.maximum(pos - win // 2, 0)
        pooled = acc / cnt.astype(F32) - v[:, cols]
        yb = jnp.dot(pooled.astype(BF16), pw_ref[gi], preferred_element_type=F32) * ps_ref[:, cols]
        cat_ref[:, half + gi * grp:half + (gi + 1) * grp] = yb.astype(BF16)
    out = h_ref[...] + jnp.dot(cat_ref[...], wout_ref[...], preferred_element_type=F32)
    o_ref[...] = out
    o2_ref[...] = out


def even_post(z, ys, d_skip, w_glu, b_glu, pool_w, pool_scale, w_out, h, seq):
    t, width = z.shape
    half = width // 2
    d = h.shape[1]
    tm = _row_tile(seq, 512)
    hb = tm // POOL_HALO
    nblk8 = t // POOL_HALO
    kern = functools.partial(_even_post_kernel, seq=seq)
    full = lambda shape: pl.BlockSpec(shape, lambda i: (0,) * len(shape))
    return pl.pallas_call(
        kern,
        out_shape=(jax.ShapeDtypeStruct((t, d), F32), jax.ShapeDtypeStruct((t, d), F32)),
        grid=(t // tm,),
        in_specs=[pl.BlockSpec((tm, width), lambda i: (i, 0)),
                  pl.BlockSpec((POOL_HALO, half), lambda i: (jnp.maximum(i * hb - 1, 0), 1)),
                  pl.BlockSpec((POOL_HALO, half), lambda i: (jnp.minimum((i + 1) * hb, nblk8 - 1), 1)),
                  pl.BlockSpec((tm, half), lambda i: (i, 0)),
                  full((1, half)), full((half, half)), full((1, half)),
                  full(pool_w.shape), full((1, half)), full((width, d)),
                  pl.BlockSpec((tm, d), lambda i: (i, 0))],
        out_specs=(pl.BlockSpec((tm, d), lambda i: (i, 0)), pl.BlockSpec((tm, d), lambda i: (i, 0))),
        scratch_shapes=[pltpu.VMEM((tm + 2 * POOL_HALO, half), F32), pltpu.VMEM((tm, width), BF16)],
        compiler_params=_params("parallel"),
        name="even_post",
    )(z, z, z, ys, d_skip.reshape(1, half), w_glu, b_glu.reshape(1, half), pool_w,
      pool_scale.reshape(1, half), w_out, h)


def _rope_kernel(pos_ref, inv_ref, sign_ref, cos_ref, sin_ref):
    ang = pos_ref[...].astype(F32) * inv_ref[...]
    cos_ref[...] = jnp.cos(ang)
    sin_ref[...] = jnp.sin(ang) * sign_ref[...]


def rope_tables(positions):
    t = positions.size
    half = ATT_HEAD_DIM // 2
    inv = ROPE_THETA ** (-jnp.arange(0, ATT_HEAD_DIM, 2, dtype=F32) / ATT_HEAD_DIM)
    reps = LANES // half
    inv_pat = jnp.tile(inv, reps).reshape(1, LANES)
    sign = jnp.tile(jnp.concatenate([-jnp.ones((half,), F32), jnp.ones((half,), F32)]), reps // 2).reshape(1, LANES)
    pos = jnp.broadcast_to(positions.reshape(t, 1), (t, LANES))
    tm = _row_tile(t, 1024)
    return pl.pallas_call(
        _rope_kernel,
        out_shape=(jax.ShapeDtypeStruct((t, LANES), F32), jax.ShapeDtypeStruct((t, LANES), F32)),
        grid=(t // tm,),
        in_specs=[pl.BlockSpec((tm, LANES), lambda i: (i, 0)),
                  pl.BlockSpec((1, LANES), lambda i: (0, 0)),
                  pl.BlockSpec((1, LANES), lambda i: (0, 0))],
        out_specs=(pl.BlockSpec((tm, LANES), lambda i: (i, 0)), pl.BlockSpec((tm, LANES), lambda i: (i, 0))),
        compiler_params=_params("parallel"),
        name="rope_tables",
    )(pos, inv_pat, sign)


ATT_KV_CHUNK = 512


def _qkv_kernel(h_ref, g_ref, wqk_ref, wvt_ref, cos_ref, sin_ref, qg_ref, kg_ref, seg_ref, q_ref, k_ref, vt_ref):
    d = h_ref.shape[1]
    y = _rmsnorm_rows(h_ref[...], g_ref[...]).astype(BF16)
    cos = cos_ref[...]
    sin = sin_ref[...]
    seg = seg_ref[...]
    first_half = lax.broadcasted_iota(jnp.int32, (1, LANES), 1) % ATT_HEAD_DIM < ATT_HEAD_DIM // 2
    q_scale = ATT_HEAD_DIM ** -0.5 * math.log2(math.e)
    for which, (o_ref, gain_ref, scale) in enumerate(((q_ref, qg_ref, q_scale), (k_ref, kg_ref, 1.0))):
        x_all = jnp.dot(y, wqk_ref[:, which * d:(which + 1) * d], preferred_element_type=F32)
        for j in range(d // LANES):
            x = x_all[:, j * LANES:(j + 1) * LANES]
            hi, lo = _split_bf16(x * x)
            ss = jnp.dot(hi, seg, preferred_element_type=F32) + jnp.dot(lo, seg, preferred_element_type=F32)
            xn = x * lax.rsqrt(ss * (1.0 / ATT_HEAD_DIM) + EPS) * gain_ref[...]
            swapped = jnp.where(first_half, pltpu.roll(xn, LANES - ATT_HEAD_DIM // 2, 1),
                                pltpu.roll(xn, ATT_HEAD_DIM // 2, 1))
            o_ref[:, j * LANES:(j + 1) * LANES] = ((xn * cos + swapped * sin) * scale).astype(BF16)
    vt = lax.dot_general(wvt_ref[...], y, (((1,), (1,)), ((), ())), preferred_element_type=F32)
    vt_ref[0] = vt.reshape(vt_ref.shape[1:]).astype(BF16)


def attn_qkv(h, g, w_qkv, cos, sin, q_gain, k_gain):
    t, d = h.shape
    tm = _row_tile(t, ATT_KV_CHUNK)
    heads = d // LANES
    reps = LANES // ATT_HEAD_DIM
    lane = jnp.arange(LANES)
    seg = (lane[:, None] // ATT_HEAD_DIM == lane[None, :] // ATT_HEAD_DIM).astype(BF16)
    w_qk = w_qkv[:, :2 * d].astype(BF16)
    w_vt = w_qkv[:, 2 * d:].T.astype(BF16)
    row = lambda n: pl.BlockSpec((tm, n), lambda i: (i, 0))
    one = lambda n: pl.BlockSpec((1, n), lambda i: (0, 0))
    out = jax.ShapeDtypeStruct((t, d), BF16)
    return pl.pallas_call(
        _qkv_kernel,
        out_shape=(out, out, jax.ShapeDtypeStruct((t // tm, heads, LANES, tm), BF16)),
        grid=(t // tm,),
        in_specs=[row(d), one(d), pl.BlockSpec(w_qk.shape, lambda i: (0, 0)), pl.BlockSpec(w_vt.shape, lambda i: (0, 0)),
                  row(LANES), row(LANES), one(LANES), one(LANES), pl.BlockSpec((LANES, LANES), lambda i: (0, 0))],
        out_specs=(row(d), row(d), pl.BlockSpec((1, heads, LANES, tm), lambda i: (i, 0, 0, 0))),
        compiler_params=_params("parallel"),
        name="attn_qkv",
    )(h, g.reshape(1, d), w_qk, w_vt, cos, sin, jnp.tile(q_gain.astype(F32), reps).reshape(1, LANES),
      jnp.tile(k_gain.astype(F32), reps).reshape(1, LANES), seg)


def _flash_kernel(q_ref, k_ref, vt_ref, lq1_ref, lk1_ref, lq2_ref, lk2_ref, sg_ref, o_ref, acc_ref, sta_ref, stb_ref,
                  *, lam_init):
    tq = q_ref.shape[0]
    nk, _, _, tk = vt_ref.shape
    q = q_ref[...]
    first = lax.broadcasted_iota(jnp.int32, (1, LANES), 1) < ATT_HEAD_DIM
    zero = jnp.zeros_like(q)
    qm = jnp.concatenate([jnp.where(first, q, zero), jnp.where(first, zero, q)], axis=0)
    acc_ref[...] = jnp.zeros(acc_ref.shape, F32)
    nt = (((1,), (1,)), ((), ()))

    def scores(i, st_ref):
        kc = k_ref[pl.ds(pl.multiple_of(i * tk, tk), tk), :]
        st_ref[...] = lax.dot_general(kc, qm, nt, preferred_element_type=F32)

    def absorb(i, st_ref, m_prev, l8):
        st = st_ref[...]
        m_new = jnp.maximum(m_prev, jnp.max(st, axis=0, keepdims=True))
        alpha = jnp.exp2(m_prev - m_new)
        p = jnp.exp2(st - m_new)
        l8 = alpha * l8 + jnp.sum(p.reshape(tk // 8, 8, 2 * tq), axis=0)
        acc_ref[...] = alpha * acc_ref[...] + jnp.dot(vt_ref[i, 0], p.astype(BF16), preferred_element_type=F32)
        return m_new, l8

    def body(j, carry):
        m, l8 = carry
        scores(2 * j + 1, stb_ref)
        m, l8 = absorb(2 * j, sta_ref, m, l8)
        scores(jnp.minimum(2 * j + 2, nk - 1), sta_ref)
        return absorb(2 * j + 1, stb_ref, m, l8)

    assert nk % 2 == 0
    scores(0, sta_ref)
    m0 = jnp.full((1, 2 * tq), -jnp.inf, F32)
    _, l8 = lax.fori_loop(0, nk // 2, body, (m0, jnp.zeros((8, 2 * tq), F32)))
    lam = (jnp.exp(jnp.sum(lq1_ref[...] * lk1_ref[...], axis=-1, keepdims=True))
           - jnp.exp(jnp.sum(lq2_ref[...] * lk2_ref[...], axis=-1, keepdims=True)) + lam_init)
    o = acc_ref[...] / jnp.sum(l8, axis=0, keepdims=True)
    o = o[:, :tq] - lam * o[:, tq:]
    o = o * lax.rsqrt(jnp.mean(o * o, axis=0, keepdims=True) + EPS) * sg_ref[...]
    o_ref[...] = (o * (1.0 - lam_init)).T.astype(o_ref.dtype)


def flash_diff_attention(q, k, vt, lam_q1, lam_k1, lam_q2, lam_k2, subln_g, n_batch, lam_init):
    t, width = q.shape
    seq = t // n_batch
    heads = width // LANES
    tk = vt.shape[-1]
    tq = _row_tile(seq, 256)
    nq, nk = seq // tq, seq // tk
    kern = functools.partial(_flash_kernel, lam_init=lam_init)
    vec = lambda n: pl.BlockSpec((1, n), lambda b, h, qi: (0, 0))
    lam_vec = lambda x: x.astype(F32).reshape(1, ATT_HEAD_DIM)
    return pl.pallas_call(
        kern,
        out_shape=jax.ShapeDtypeStruct((t, width), BF16),
        grid=(n_batch, heads, nq),
        in_specs=[pl.BlockSpec((tq, LANES), lambda b, h, qi: (b * nq + qi, h)),
                  pl.BlockSpec((seq, LANES), lambda b, h, qi: (b, h)),
                  pl.BlockSpec((nk, 1, LANES, tk), lambda b, h, qi: (b, h, 0, 0)),
                  vec(ATT_HEAD_DIM), vec(ATT_HEAD_DIM), vec(ATT_HEAD_DIM), vec(ATT_HEAD_DIM),
                  pl.BlockSpec((LANES, 1), lambda b, h, qi: (0, 0))],
        out_specs=pl.BlockSpec((tq, LANES), lambda b, h, qi: (b * nq + qi, h)),
        scratch_shapes=[pltpu.VMEM((LANES, 2 * tq), F32), pltpu.VMEM((tk, 2 * tq), F32),
                        pltpu.VMEM((tk, 2 * tq), F32)],
        compiler_params=_params("parallel", "parallel", "arbitrary"),
        name="flash_diff_attention",
    )(q, k, vt, lam_vec(lam_q1), lam_vec(lam_k1), lam_vec(lam_q2), lam_vec(lam_k2),
      subln_g.astype(F32).reshape(LANES, 1))


def _proj_residual_kernel(x_ref, w_ref, h_ref, o_ref, o2_ref):
    out = h_ref[...] + jnp.dot(x_ref[...], w_ref[...], preferred_element_type=F32)
    o_ref[...] = out
    o2_ref[...] = out


def proj_residual(x, w, h):
    t, d = h.shape
    kdim = x.shape[1]
    tm = _row_tile(t, 512)
    row = lambda n: pl.BlockSpec((tm, n), lambda i: (i, 0))
    out = jax.ShapeDtypeStruct((t, d), F32)
    return pl.pallas_call(
        _proj_residual_kernel,
        out_shape=(out, out),
        grid=(t // tm,),
        in_specs=[row(kdim), pl.BlockSpec((kdim, d), lambda i: (0, 0)), row(d)],
        out_specs=(row(d), row(d)),
        compiler_params=_params("parallel"),
        name="proj_residual",
    )(x, w, h)


def _router_kernel(h_ref, g_ref, wt_hi_ref, wt_lo_ref, aff_ref):
    y = _rmsnorm_rows(h_ref[...], g_ref[...])
    y_hi, y_lo = _split_bf16(y)
    nt = (((1,), (1,)), ((), ()))
    logits = (lax.dot_general(wt_hi_ref[...], y_hi, nt, preferred_element_type=F32)
              + lax.dot_general(wt_hi_ref[...], y_lo, nt, preferred_element_type=F32)
              + lax.dot_general(wt_lo_ref[...], y_hi, nt, preferred_element_type=F32))
    e = jnp.exp(logits - jnp.max(logits, axis=0, keepdims=True))
    aff_ref[0] = e / jnp.sum(e, axis=0, keepdims=True)


def router(h, g, w_router, n_batch):
    t, d = h.shape
    seq = t // n_batch
    n_exp = w_router.shape[1]
    tm = _row_tile(seq, 512)
    ns = seq // tm
    wt = w_router.astype(F32).T
    wt_hi, wt_lo = _split_bf16(wt)
    return pl.pallas_call(
        _router_kernel,
        out_shape=jax.ShapeDtypeStruct((n_batch, n_exp, seq), F32),
        grid=(n_batch, ns),
        in_specs=[pl.BlockSpec((tm, d), lambda b, i: (b * ns + i, 0)),
                  pl.BlockSpec((1, d), lambda b, i: (0, 0)),
                  pl.BlockSpec((n_exp, d), lambda b, i: (0, 0)),
                  pl.BlockSpec((n_exp, d), lambda b, i: (0, 0))],
        out_specs=pl.BlockSpec((1, n_exp, tm), lambda b, i: (b, 0, i)),
        compiler_params=_params("parallel", "parallel"),
        name="router",
    )(h, g.reshape(1, d), wt_hi, wt_lo)


def _select_kernel(aff_ref, affw_ref, idx_ref, gate_ref, *, cap):
    rows_all = aff_ref.shape[1]
    n_exp = idx_ref.shape[1]
    r = rows_all // n_exp
    aff = aff_ref[0]
    bits_wide = pltpu.bitcast(affw_ref[0], jnp.int32)

    def count(mask):
        return jnp.sum(jnp.where(mask, 1.0, 0.0), axis=1, keepdims=True)

    def bit_step(i, thr):
        cand = thr | jnp.left_shift(jnp.int32(1), 30 - i)
        return jnp.where(count(bits_wide >= cand) >= cap, cand, thr)

    thr = lax.fori_loop(0, 31, bit_step, jnp.zeros((n_exp, 1), jnp.int32))
    n_tie_take = cap - count(bits_wide > thr)

    tri_incl = (lax.broadcasted_iota(jnp.int32, (LANES, LANES), 0)
                <= lax.broadcasted_iota(jnp.int32, (LANES, LANES), 1)).astype(BF16)
    ones_mat = jnp.ones((LANES, LANES), BF16)
    low_strict = (lax.broadcasted_iota(jnp.int32, (r, r), 1)
                  < lax.broadcasted_iota(jnp.int32, (r, r), 0)).astype(BF16)

    def prefix(mask2d):
        m = jnp.where(mask2d, 1.0, 0.0).astype(BF16)
        lane_incl = jnp.dot(m, tri_incl, preferred_element_type=F32)
        row_tot = jnp.dot(m, ones_mat, preferred_element_type=F32)
        row_off = jnp.dot(low_strict, row_tot.astype(BF16), preferred_element_type=F32)
        return lane_incl, row_tot, row_off

    slot = lax.broadcasted_iota(jnp.int32, (1, cap), 1).astype(F32)
    row_id = lax.broadcasted_iota(jnp.int32, (r, cap), 0).astype(F32)
    lane_id = lax.broadcasted_iota(jnp.int32, (LANES, cap), 0).astype(F32)
    tn = (((0,), (0,)), ((), ()))
    for e in range(n_exp):
        a_e = aff[e * r:(e + 1) * r, :]
        bits = pltpu.bitcast(a_e, jnp.int32)
        thr_e = thr[e:e + 1, :]
        tie_e = bits == thr_e
        t_incl, _, t_off = prefix(tie_e)
        tie_rank = t_off + t_incl - 1.0
        take = jnp.where(tie_rank < n_tie_take[e:e + 1, :], 1.0, 0.0) * jnp.where(tie_e, 1.0, 0.0)
        sel = jnp.where(bits > thr_e, 1.0, take) > 0.5
        lane_incl, row_tot, row_off = prefix(sel)
        off_b = jnp.tile(row_off, (1, cap // LANES))
        cum_b = jnp.tile(row_off + row_tot, (1, cap // LANES))
        row_of = jnp.sum(jnp.where(cum_b <= slot, 1.0, 0.0), axis=0, keepdims=True)
        off_of = jnp.max(jnp.where(off_b <= slot, off_b, 0.0), axis=0, keepdims=True)
        local = slot - off_of
        onehot = jnp.where(row_id == row_of, 1.0, 0.0).astype(BF16)
        incl_t = lax.dot_general(lane_incl.astype(BF16), onehot, tn, preferred_element_type=F32)
        lane_of = jnp.sum(jnp.where(incl_t <= local, 1.0, 0.0), axis=0, keepdims=True)
        idx_ref[0, e:e + 1, :] = (row_of * LANES + lane_of).astype(jnp.int32)
        a_hi = a_e.astype(BF16)
        a_mid = (a_e - a_hi.astype(F32)).astype(BF16)
        a_lo = (a_e - a_hi.astype(F32) - a_mid.astype(F32)).astype(BF16)
        a_t = (lax.dot_general(a_hi, onehot, tn, preferred_element_type=F32)
               + lax.dot_general(a_mid, onehot, tn, preferred_element_type=F32)
               + lax.dot_general(a_lo, onehot, tn, preferred_element_type=F32))
        gate_ref[0, e:e + 1, :] = jnp.sum(jnp.where(lane_id == lane_of, a_t, 0.0), axis=0, keepdims=True)


def expert_select(aff_t, cap):
    n_batch, n_exp, seq = aff_t.shape
    rows = n_exp * seq // LANES
    kern = functools.partial(_select_kernel, cap=cap)
    return pl.pallas_call(
        kern,
        out_shape=(jax.ShapeDtypeStruct((n_batch, n_exp, cap), jnp.int32),
                   jax.ShapeDtypeStruct((n_batch, n_exp, cap), F32)),
        grid=(n_batch,),
        in_specs=[pl.BlockSpec((1, rows, LANES), lambda b: (b, 0, 0)),
                  pl.BlockSpec((1, n_exp, seq), lambda b: (b, 0, 0))],
        out_specs=(pl.BlockSpec((1, n_exp, cap), lambda b: (b, 0, 0)),
                   pl.BlockSpec((1, n_exp, cap), lambda b: (b, 0, 0))),
        compiler_params=_params("parallel"),
        name="expert_select",
    )(aff_t.reshape(n_batch, rows, LANES), aff_t)


DMA_UNROLL = 8


def _cast_kernel(x_ref, o_ref):
    o_ref[...] = x_ref[...].astype(o_ref.dtype)


def cast_layer_bf16(w, layer):
    _, n, r, c = w.shape
    return pl.pallas_call(
        _cast_kernel,
        out_shape=jax.ShapeDtypeStruct((n, r, c), BF16),
        grid=(n,),
        in_specs=[pl.BlockSpec((None, 1, r, c), lambda i: (layer, i, 0, 0))],
        out_specs=pl.BlockSpec((1, r, c), lambda i: (i, 0, 0)),
        compiler_params=_params("parallel"),
        name="cast_bf16",
    )(w)


def _moe_kernel(idx_ref, gate_ref, g_ref, wg_ref, wu_ref, wd_ref, hin_ref, acc_in_ref, acc_ref,
                x_buf, o_buf, sem_x, sem_o, sem_s, *, seq, cap, n_batch):
    del acc_in_ref
    e, b, ti = pl.program_id(0), pl.program_id(1), pl.program_id(2)
    n_exp, nb, nt = pl.num_programs(0), pl.num_programs(1), pl.num_programs(2)
    tm = x_buf.shape[1]
    step = (e * nb + b) * nt + ti
    last = n_exp * nb * nt - 1
    slot = step % 2
    other = 1 - slot
    nxt = jnp.minimum(step + 1, last)
    e_n, b_n, ti_n = nxt // (nb * nt), (nxt // nt) % nb, nxt % nt
    base = (b * n_exp + e) * cap + ti * tm
    base_n = (b_n * n_exp + e_n) * cap + ti_n * tm

    def x_copy(j, src_base, src_batch, dst_slot):
        row = pl.ds(src_batch * seq + idx_ref[src_base + j], 1)
        return pltpu.make_async_copy(hin_ref.at[row], x_buf.at[dst_slot, pl.ds(j, 1)], sem_x.at[dst_slot])

    def acc_rows(j):
        return acc_ref.at[pl.ds(b * seq + idx_ref[base + j], 1)]

    def wait_x(s):
        pltpu.make_async_copy(hin_ref.at[pl.ds(0, tm)], x_buf.at[s], sem_x.at[s]).wait()

    def wait_scatter(s):
        pltpu.make_async_copy(o_buf.at[s], acc_ref.at[pl.ds(0, tm)], sem_s.at[s]).wait()

    @pl.when(step == 0)
    def _():
        def first(jj, c):
            for u in range(DMA_UNROLL):
                x_copy(jj * DMA_UNROLL + u, base, b, slot).start()
            return c
        lax.fori_loop(0, tm // DMA_UNROLL, first, 0)

    if n_batch == 1:
        @pl.when(step > 0)
        def _():
            wait_scatter(other)

    wait_x(slot)
    for j in range(tm):
        x_copy(j, base_n, b_n, other).start()
        pltpu.make_async_copy(acc_rows(j), o_buf.at[slot, pl.ds(j, 1)], sem_o.at[slot]).start()
    x = _rmsnorm_rows(x_buf[slot], g_ref[...]).astype(BF16)
    a = jnp.dot(x, wg_ref[0], preferred_element_type=F32)
    u = jnp.dot(x, wu_ref[0], preferred_element_type=F32)
    hmid = (a * (1.0 / (1.0 + jnp.exp(-a))) * u).astype(BF16)
    y = jnp.dot(hmid, wd_ref[0], preferred_element_type=F32)
    pltpu.make_async_copy(acc_ref.at[pl.ds(0, tm)], o_buf.at[slot], sem_o.at[slot]).wait()
    o_buf[slot] = o_buf[slot] + y * gate_ref[...]

    def start_scatter(jj, c):
        for u in range(DMA_UNROLL):
            j = jj * DMA_UNROLL + u
            pltpu.make_async_copy(o_buf.at[slot, pl.ds(j, 1)], acc_rows(j), sem_s.at[slot]).start()
        return c

    lax.fori_loop(0, tm // DMA_UNROLL, start_scatter, 0)

    if n_batch > 1:
        @pl.when(step > 0)
        def _():
            wait_scatter(other)

    @pl.when(step == last)
    def _():
        wait_x(other)
        wait_scatter(slot)


def moe_ffn(h, acc, g, idx, gate, w_gate, w_up, w_down, n_batch):
    t, d = h.shape
    seq = t // n_batch
    n_exp, _, ff = w_gate.shape
    cap = idx.shape[-1]
    tm = _row_tile(cap, 512)
    nt = cap // tm
    kern = functools.partial(_moe_kernel, seq=seq, cap=cap, n_batch=n_batch)
    grid_spec = pltpu.PrefetchScalarGridSpec(
        num_scalar_prefetch=1,
        grid=(n_exp, n_batch, nt),
        in_specs=[pl.BlockSpec((tm, 1), lambda e, b, i, idx: ((b * n_exp + e) * nt + i, 0)),
                  pl.BlockSpec((1, d), lambda e, b, i, idx: (0, 0)),
                  pl.BlockSpec((1, d, ff), lambda e, b, i, idx: (e, 0, 0)),
                  pl.BlockSpec((1, d, ff), lambda e, b, i, idx: (e, 0, 0)),
                  pl.BlockSpec((1, ff, d), lambda e, b, i, idx: (e, 0, 0)),
                  pl.BlockSpec(memory_space=pl.ANY),
                  pl.BlockSpec(memory_space=pl.ANY)],
        out_specs=pl.BlockSpec(memory_space=pl.ANY),
        scratch_shapes=[pltpu.VMEM((2, tm, d), F32), pltpu.VMEM((2, tm, d), F32), pltpu.SemaphoreType.DMA((2,)),
                        pltpu.SemaphoreType.DMA((2,)), pltpu.SemaphoreType.DMA((2,))],
    )
    return pl.pallas_call(
        kern,
        out_shape=jax.ShapeDtypeStruct((t, d), F32),
        grid_spec=grid_spec,
        input_output_aliases={7: 0},
        compiler_params=_params("arbitrary", "arbitrary", "arbitrary"),
        name="moe_ffn",
    )(idx.reshape(-1), gate.reshape(-1, 1), g.reshape(1, d), w_gate, w_up, w_down, h, acc)


def _even_layer(h, n_batch, seq, g_mix, w_in, w_out, s5, d_skip, w_glu, b_glu, pool_w, pool_scale):
    t = h.shape[0]
    z = norm_proj(h, g_mix, w_in.astype(BF16), F32)
    half = z.shape[1] // 2
    n_grp = half // S5_GROUP
    nc = t // S5_CHUNK
    u = z[:, :half].astype(BF16).reshape(nc, S5_CHUNK, n_grp, S5_GROUP)
    u = u.transpose(2, 0, 1, 3).reshape(n_grp, nc, S5_CHUNK * S5_GROUP)
    wt, wst, wout, al = s5_chunk_matrices(*s5)
    ys = s5_scan(u, wt.astype(BF16), wst.astype(BF16), wout.astype(BF16), al, n_batch)
    ys = ys.reshape(n_grp, nc, S5_CHUNK, S5_GROUP).transpose(1, 2, 0, 3).reshape(t, half)
    return even_post(z, ys, d_skip.astype(F32), w_glu.astype(BF16), b_glu.astype(F32), pool_w.astype(BF16),
                     pool_scale.astype(F32), w_out.astype(BF16), h, seq)


def _odd_layer(h, n_batch, layer, g_mix, cos, sin, w_qkv, w_out, q_gain, k_gain, lq1, lk1, lq2, lk2, subln_g):
    q, k, vt = attn_qkv(h, g_mix, w_qkv, cos, sin, q_gain, k_gain)
    lam_init = 0.8 - 0.6 * math.exp(-0.3 * layer)
    o = flash_diff_attention(q, k, vt, lq1, lk1, lq2, lk2, subln_g, n_batch, lam_init)
    return proj_residual(o, w_out.astype(BF16), h)


def _moe_layer(h, acc, n_batch, layer, g_ffn, w_router, w_gate_all, w_up_all, w_down_all):
    seq = h.shape[0] // n_batch
    cap = CAPACITY_FACTOR * seq // N_EXPERTS
    aff_t = router(h, g_ffn, w_router, n_batch)
    idx, gate = expert_select(aff_t, cap)
    return moe_ffn(h, acc, g_ffn, idx, gate, cast_layer_bf16(w_gate_all, layer), cast_layer_bf16(w_up_all, layer),
                   cast_layer_bf16(w_down_all, layer), n_batch)


def kernel(x, positions, norm_mix_g, norm_ffn_g, hyb_w_in, hyb_w_out, s5_lam_re, s5_lam_im, s5_log_dt, s5_b_re, s5_b_im, s5_c_re, s5_c_im, s5_d, s5_w_glu, s5_b_glu, pool_w, pool_scale, attn_w_qkv, attn_w_out, attn_q_norm_g, attn_k_norm_g, attn_lam_q1, attn_lam_k1, attn_lam_q2, attn_lam_k2, attn_subln_g, moe_w_router, moe_w_gate, moe_w_up, moe_w_down):
    n_batch, seq, d = x.shape
    depth = norm_mix_g.shape[0]
    h = x.reshape(n_batch * seq, d)
    cos, sin = rope_tables(positions)
    for layer in range(depth):
        if layer % 2 == 0:
            e = layer // 2
            s5 = (s5_lam_re[e], s5_lam_im[e], s5_log_dt[e], s5_b_re[e], s5_b_im[e], s5_c_re[e], s5_c_im[e])
            h, acc = _even_layer(h, n_batch, seq, norm_mix_g[layer], hyb_w_in[e], hyb_w_out[e], s5, s5_d[e],
                                 s5_w_glu[e], s5_b_glu[e], pool_w[e], pool_scale[e])
        else:
            o = layer // 2
            h, acc = _odd_layer(h, n_batch, layer, norm_mix_g[layer], cos, sin, attn_w_qkv[o], attn_w_out[o],
                                attn_q_norm_g[o], attn_k_norm_g[o], attn_lam_q1[o], attn_lam_k1[o],
                                attn_lam_q2[o], attn_lam_k2[o], attn_subln_g[o])
        h = _moe_layer(h, acc, n_batch, layer, norm_ffn_g[layer], moe_w_router[layer], moe_w_gate,
                       moe_w_up, moe_w_down)
    return h.reshape(n_batch, seq, d)
```

```python
import functools
import math

import jax
import jax.numpy as jnp
from jax import lax
from jax.experimental import pallas as pl
from jax.experimental.pallas import tpu as pltpu

F32 = jnp.float32
BF16 = jnp.bfloat16

EPS = 1e-6
LANES = 128
VMEM_LIMIT = 56 * 1024 * 1024

S5_GROUP = 16
S5_CHUNK = 16
POOL_WINDOWS = (2, 4, 8, 16)
POOL_HALO = 8
ATT_HEAD_DIM = 64
ROPE_THETA = 10000.0
N_EXPERTS = 16
CAPACITY_FACTOR = 2


def _row_tile(n, want):
    t = min(n, want)
    assert n % t == 0, (n, t)
    return t


def _params(*sem):
    return pltpu.CompilerParams(dimension_semantics=sem, vmem_limit_bytes=VMEM_LIMIT)


def _rmsnorm_rows(x, g):
    return x * lax.rsqrt(jnp.mean(x * x, axis=-1, keepdims=True) + EPS) * g


def _split_bf16(x):
    hi = x.astype(BF16)
    lo = (x - hi.astype(F32)).astype(BF16)
    return hi, lo


def _norm_proj_kernel(h_ref, g_ref, w_ref, o_ref):
    y = _rmsnorm_rows(h_ref[...], g_ref[...])
    o_ref[...] = jnp.dot(y.astype(BF16), w_ref[...], preferred_element_type=F32).astype(o_ref.dtype)


def norm_proj(h, g, w, out_dtype):
    t, d = h.shape
    n = w.shape[1]
    tm = _row_tile(t, 512)
    return pl.pallas_call(
        _norm_proj_kernel,
        out_shape=jax.ShapeDtypeStruct((t, n), out_dtype),
        grid=(t // tm,),
        in_specs=[pl.BlockSpec((tm, d), lambda i: (i, 0)),
                  pl.BlockSpec((1, d), lambda i: (0, 0)),
                  pl.BlockSpec((d, n), lambda i: (0, 0))],
        out_specs=pl.BlockSpec((tm, n), lambda i: (i, 0)),
        compiler_params=_params("parallel"),
        name="norm_proj",
    )(h, g.reshape(1, d), w)


def _s5_kernel(u_ref, wt_ref, wst_ref, wout_ref, al_ref, y_ref, s_ref, hf_ref, hb_ref, *, n_batch):
    nc_all = u_ref.shape[1]
    nc = nc_all // n_batch
    u = u_ref[0]
    s_ref[...] = jnp.dot(u, wst_ref[0], preferred_element_type=F32)
    al = al_ref[0]
    a_re = al[0:1, :]
    a_im = al[1:2, :]
    is_fwd = lax.broadcasted_iota(jnp.int32, (1, 2 * LANES), 1) % LANES < LANES // 2

    def step(i, carry):
        new = []
        for b in range(n_batch):
            h = carry[b]
            row_f = b * nc + i
            row_b = b * nc + nc - 1 - i
            hf_ref[pl.ds(row_f, 1), :] = h
            hb_ref[pl.ds(row_b, 1), :] = h
            s = jnp.where(is_fwd, s_ref[pl.ds(row_f, 1), :], s_ref[pl.ds(row_b, 1), :])
            h_re = h[:, :LANES]
            h_im = h[:, LANES:]
            n_re = a_re * h_re - a_im * h_im + s[:, :LANES]
            n_im = a_re * h_im + a_im * h_re + s[:, LANES:]
            new.append(jnp.concatenate([n_re, n_im], axis=1))
        return tuple(new)

    zero = jnp.zeros((1, 2 * LANES), F32)
    lax.fori_loop(0, nc, step, tuple(zero for _ in range(n_batch)))
    h_in = jnp.where(is_fwd, hf_ref[...], hb_ref[...])
    y = jnp.dot(u, wt_ref[0], preferred_element_type=F32)
    y = y + jnp.dot(h_in.astype(BF16), wout_ref[0], preferred_element_type=F32)
    y_ref[0] = y.astype(y_ref.dtype)


def s5_scan(u, wt, wst, wout, al, n_batch):
    g, nc_all, w = u.shape
    kern = functools.partial(_s5_kernel, n_batch=n_batch)
    mat = pl.BlockSpec((1, w, w), lambda i: (i, 0, 0))
    return pl.pallas_call(
        kern,
        out_shape=jax.ShapeDtypeStruct((g, nc_all, w), BF16),
        grid=(g,),
        in_specs=[pl.BlockSpec((1, nc_all, w), lambda i: (i, 0, 0)), mat, mat, mat,
                  pl.BlockSpec((1, 2, LANES), lambda i: (i, 0, 0))],
        out_specs=pl.BlockSpec((1, nc_all, w), lambda i: (i, 0, 0)),
        scratch_shapes=[pltpu.VMEM((nc_all, w), F32), pltpu.VMEM((nc_all, w), F32),
                        pltpu.VMEM((nc_all, w), F32)],
        compiler_params=_params("parallel"),
        name="s5_scan",
    )(u, wt, wst, wout, al)


def s5_chunk_matrices(lam_re, lam_im, log_dt, b_re, b_im, c_re, c_im):
    L = S5_CHUNK
    lam = lax.complex(lam_re.astype(F32), lam_im.astype(F32))
    dt = jnp.exp(log_dt.astype(F32))[..., None]
    lam_dt = lam * dt
    a_bar = jnp.exp(lam_dt)
    b = lax.complex(b_re.astype(F32), b_im.astype(F32))
    b_bar = ((a_bar - 1.0) / lam)[..., None] * b
    c = lax.complex(c_re.astype(F32), c_im.astype(F32))
    k = jnp.arange(L + 1, dtype=F32)
    pw = jnp.exp(lam_dt[:, :, None, :] * k[None, None, :, None])
    n_grp, n_state = lam.shape[1], lam.shape[2]
    kern = jnp.einsum('dgop,dgkp,dgpi->dgkoi', c, pw[:, :, :L], b_bar).real
    t_idx = jnp.arange(L)[None, :]
    s_idx = jnp.arange(L)[:, None]
    lag_f = t_idx - s_idx
    kf = jnp.where((lag_f >= 0)[None, :, :, None, None], kern[0][:, jnp.clip(lag_f, 0, L - 1)], 0.0)
    kb = jnp.where((lag_f <= 0)[None, :, :, None, None], kern[1][:, jnp.clip(-lag_f, 0, L - 1)], 0.0)
    wt = (kf + kb).transpose(0, 1, 4, 2, 3).reshape(n_grp, L * S5_GROUP, L * S5_GROUP)
    cf = pw[0][:, L - 1 - jnp.arange(L)][..., None] * b_bar[0][:, None]
    cb = pw[1][:, jnp.arange(L)][..., None] * b_bar[1][:, None]
    def st(x):
        return x.transpose(0, 1, 3, 2).reshape(n_grp, L * S5_GROUP, n_state)
    wst = jnp.concatenate([st(cf.real), st(cb.real), st(cf.imag), st(cb.imag)], axis=-1)
    of = c[0][:, None] * pw[0][:, 1 + jnp.arange(L)][:, :, None, :]
    ob = c[1][:, None] * pw[1][:, L - jnp.arange(L)][:, :, None, :]
    def ot(x):
        return x.transpose(0, 3, 1, 2).reshape(n_grp, n_state, L * S5_GROUP)
    wout = jnp.concatenate([ot(of.real), ot(ob.real), ot(-of.imag), ot(-ob.imag)], axis=1)
    a_l = pw[:, :, L]
    al = jnp.stack([jnp.concatenate([a_l[0].real, a_l[1].real], axis=-1),
                    jnp.concatenate([a_l[0].imag, a_l[1].imag], axis=-1)], axis=1)
    return wt, wst, wout, al


def _gelu_tanh(x):
    return 0.5 * x * (1.0 + jnp.tanh(math.sqrt(2.0 / math.pi) * (x + 0.044715 * (x * x * x))))


def _even_post_kernel(z_ref, zp_ref, zn_ref, ys_ref, d_ref, wglu_ref, bglu_ref, pw_ref, ps_ref, wout_ref,
                      h_ref, o_ref, o2_ref, ext_ref, cat_ref, *, seq):
    tm = z_ref.shape[0]
    half = z_ref.shape[1] // 2
    t0 = (pl.program_id(0) * tm) % seq
    y = d_ref[...] * z_ref[:, :half] + ys_ref[...].astype(F32)
    y = _gelu_tanh(y)
    gate = jnp.dot(y.astype(BF16), wglu_ref[...], preferred_element_type=F32) + bglu_ref[...]
    cat_ref[:, :half] = (y * (1.0 / (1.0 + jnp.exp(-gate)))).astype(BF16)
    v = z_ref[:, half:]
    ext_ref[0:POOL_HALO, :] = jnp.where(t0 > 0, zp_ref[...], 0.0)
    ext_ref[POOL_HALO:POOL_HALO + tm, :] = v
    ext_ref[POOL_HALO + tm:, :] = jnp.where(t0 + tm < seq, zn_ref[...], 0.0)
    pos = t0 + lax.broadcasted_iota(jnp.int32, (tm, 1), 0)
    grp = half // len(POOL_WINDOWS)
    for gi, win in enumerate(POOL_WINDOWS):
        cols = slice(gi * grp, (gi + 1) * grp)
        acc = ext_ref[POOL_HALO - win // 2:POOL_HALO - win // 2 + tm, cols]
        for j in range(1, win):
            off = POOL_HALO - win // 2 + j
            acc = acc + ext_ref[off:off + tm, cols]
        cnt = jnp.minimum(pos + win // 2, seq) - jnp.maximum(pos - win // 2, 0)
        pooled = acc / cnt.astype(F32) - v[:, cols]
        yb = jnp.dot(pooled.astype(BF16), pw_ref[gi], preferred_element_type=F32) * ps_ref[:, cols]
        cat_ref[:, half + gi * grp:half + (gi + 1) * grp] = yb.astype(BF16)
    out = h_ref[...] + jnp.dot(cat_ref[...], wout_ref[...], preferred_element_type=F32)
    o_ref[...] = out
    o2_ref[...] = out


def even_post(z, ys, d_skip, w_glu, b_glu, pool_w, pool_scale, w_out, h, seq):
    t, width = z.shape
    half = width // 2
    d = h.shape[1]
    tm = _row_tile(seq, 512)
    hb = tm // POOL_HALO
    nblk8 = t // POOL_HALO
    kern = functools.partial(_even_post_kernel, seq=seq)
    full = lambda shape: pl.BlockSpec(shape, lambda i: (0,) * len(shape))
    return pl.pallas_call(
        kern,
        out_shape=(jax.ShapeDtypeStruct((t, d), F32), jax.ShapeDtypeStruct((t, d), F32)),
        grid=(t // tm,),
        in_specs=[pl.BlockSpec((tm, width), lambda i: (i, 0)),
                  pl.BlockSpec((POOL_HALO, half), lambda i: (jnp.maximum(i * hb - 1, 0), 1)),
                  pl.BlockSpec((POOL_HALO, half), lambda i: (jnp.minimum((i + 1) * hb, nblk8 - 1), 1)),
                  pl.BlockSpec((tm, half), lambda i: (i, 0)),
                  full((1, half)), full((half, half)), full((1, half)),
                  full(pool_w.shape), full((1, half)), full((width, d)),
                  pl.BlockSpec((tm, d), lambda i: (i, 0))],
        out_specs=(pl.BlockSpec((tm, d), lambda i: (i, 0)), pl.BlockSpec((tm, d), lambda i: (i, 0))),
        scratch_shapes=[pltpu.VMEM((tm + 2 * POOL_HALO, half), F32), pltpu.VMEM((tm, width), BF16)],
        compiler_params=_params("parallel"),
        name="even_post",
    )(z, z, z, ys, d_skip.reshape(1, half), w_glu, b_glu.reshape(1, half), pool_w,
      pool_scale.reshape(1, half), w_out, h)


def _rope_kernel(pos_ref, inv_ref, sign_ref, cos_ref, sin_ref):
    ang = pos_ref[...].astype(F32) * inv_ref[...]
    cos_ref[...] = jnp.cos(ang)
    sin_ref[...] = jnp.sin(ang) * sign_ref[...]


def rope_tables(positions):
    t = positions.size
    half = ATT_HEAD_DIM // 2
    inv = ROPE_THETA ** (-jnp.arange(0, ATT_HEAD_DIM, 2, dtype=F32) / ATT_HEAD_DIM)
    reps = LANES // half
    inv_pat = jnp.tile(inv, reps).reshape(1, LANES)
    sign = jnp.tile(jnp.concatenate([-jnp.ones((half,), F32), jnp.ones((half,), F32)]), reps // 2).reshape(1, LANES)
    pos = jnp.broadcast_to(positions.reshape(t, 1), (t, LANES))
    tm = _row_tile(t, 1024)
    return pl.pallas_call(
        _rope_kernel,
        out_shape=(jax.ShapeDtypeStruct((t, LANES), F32), jax.ShapeDtypeStruct((t, LANES), F32)),
        grid=(t // tm,),
        in_specs=[pl.BlockSpec((tm, LANES), lambda i: (i, 0)),
                  pl.BlockSpec((1, LANES), lambda i: (0, 0)),
                  pl.BlockSpec((1, LANES), lambda i: (0, 0))],
        out_specs=(pl.BlockSpec((tm, LANES), lambda i: (i, 0)), pl.BlockSpec((tm, LANES), lambda i: (i, 0))),
        compiler_params=_params("parallel"),
        name="rope_tables",
    )(pos, inv_pat, sign)


ATT_KV_CHUNK = 512


def _qkv_kernel(h_ref, g_ref, wqk_ref, wvt_ref, cos_ref, sin_ref, qg_ref, kg_ref, seg_ref, q_ref, k_ref, vt_ref):
    d = h_ref.shape[1]
    y = _rmsnorm_rows(h_ref[...], g_ref[...]).astype(BF16)
    cos = cos_ref[...]
    sin = sin_ref[...]
    seg = seg_ref[...]
    first_half = lax.broadcasted_iota(jnp.int32, (1, LANES), 1) % ATT_HEAD_DIM < ATT_HEAD_DIM // 2
    q_scale = ATT_HEAD_DIM ** -0.5 * math.log2(math.e)
    for which, (o_ref, gain_ref, scale) in enumerate(((q_ref, qg_ref, q_scale), (k_ref, kg_ref, 1.0))):
        x_all = jnp.dot(y, wqk_ref[:, which * d:(which + 1) * d], preferred_element_type=F32)
        for j in range(d // LANES):
            x = x_all[:, j * LANES:(j + 1) * LANES]
            hi, lo = _split_bf16(x * x)
            ss = jnp.dot(hi, seg, preferred_element_type=F32) + jnp.dot(lo, seg, preferred_element_type=F32)
            xn = x * lax.rsqrt(ss * (1.0 / ATT_HEAD_DIM) + EPS) * gain_ref[...]
            swapped = jnp.where(first_half, pltpu.roll(xn, LANES - ATT_HEAD_DIM // 2, 1),
                                pltpu.roll(xn, ATT_HEAD_DIM // 2, 1))
            o_ref[:, j * LANES:(j + 1) * LANES] = ((xn * cos + swapped * sin) * scale).astype(BF16)
    vt = lax.dot_general(wvt_ref[...], y, (((1,), (1,)), ((), ())), preferred_element_type=F32)
    vt_ref[0] = vt.reshape(vt_ref.shape[1:]).astype(BF16)


def attn_qkv(h, g, w_qkv, cos, sin, q_gain, k_gain):
    t, d = h.shape
    tm = _row_tile(t, ATT_KV_CHUNK)
    heads = d // LANES
    reps = LANES // ATT_HEAD_DIM
    lane = jnp.arange(LANES)
    seg = (lane[:, None] // ATT_HEAD_DIM == lane[None, :] // ATT_HEAD_DIM).astype(BF16)
    w_qk = w_qkv[:, :2 * d].astype(BF16)
    w_vt = w_qkv[:, 2 * d:].T.astype(BF16)
    row = lambda n: pl.BlockSpec((tm, n), lambda i: (i, 0))
    one = lambda n: pl.BlockSpec((1, n), lambda i: (0, 0))
    out = jax.ShapeDtypeStruct((t, d), BF16)
    return pl.pallas_call(
        _qkv_kernel,
        out_shape=(out, out, jax.ShapeDtypeStruct((t // tm, heads, LANES, tm), BF16)),
        grid=(t // tm,),
        in_specs=[row(d), one(d), pl.BlockSpec(w_qk.shape, lambda i: (0, 0)), pl.BlockSpec(w_vt.shape, lambda i: (0, 0)),
                  row(LANES), row(LANES), one(LANES), one(LANES), pl.BlockSpec((LANES, LANES), lambda i: (0, 0))],
        out_specs=(row(d), row(d), pl.BlockSpec((1, heads, LANES, tm), lambda i: (i, 0, 0, 0))),
        compiler_params=_params("parallel"),
        name="attn_qkv",
    )(h, g.reshape(1, d), w_qk, w_vt, cos, sin, jnp.tile(q_gain.astype(F32), reps).reshape(1, LANES),
      jnp.tile(k_gain.astype(F32), reps).reshape(1, LANES), seg)


def _flash_kernel(q_ref, k_ref, vt_ref, lq1_ref, lk1_ref, lq2_ref, lk2_ref, sg_ref, o_ref, acc_ref, sta_ref, stb_ref,
                  *, lam_init):
    tq = q_ref.shape[0]
    nk, _, _, tk = vt_ref.shape
    q = q_ref[...]
    first = lax.broadcasted_iota(jnp.int32, (1, LANES), 1) < ATT_HEAD_DIM
    zero = jnp.zeros_like(q)
    qm = jnp.concatenate([jnp.where(first, q, zero), jnp.where(first, zero, q)], axis=0)
    acc_ref[...] = jnp.zeros(acc_ref.shape, F32)
    nt = (((1,), (1,)), ((), ()))

    def scores(i, st_ref):
        kc = k_ref[pl.ds(pl.multiple_of(i * tk, tk), tk), :]
        st = lax.dot_general(kc, qm, nt, preferred_element_type=F32)
        st_ref[...] = st
        return jnp.max(st, axis=0, keepdims=True)

    def absorb(i, st_ref, chunk_max, m_prev, l8):
        m_new = jnp.maximum(m_prev, chunk_max)
        alpha = jnp.exp2(m_prev - m_new)
        p = jnp.exp2(st_ref[...] - m_new)
        l8 = alpha * l8 + jnp.sum(p.reshape(tk // 8, 8, 2 * tq), axis=0)
        acc_ref[...] = alpha * acc_ref[...] + jnp.dot(vt_ref[i, 0], p.astype(BF16), preferred_element_type=F32)
        return m_new, l8

    per_trip = 8 if nk % 8 == 0 else 2
    assert nk % per_trip == 0

    def body(j, carry):
        m, l8, max_cur = carry
        bufs = (sta_ref, stb_ref)
        for u in range(per_trip):
            i = per_trip * j + u
            nxt = i + 1 if u + 1 < per_trip else jnp.minimum(i + 1, nk - 1)
            max_nxt = scores(nxt, bufs[(u + 1) % 2])
            m, l8 = absorb(i, bufs[u % 2], max_cur, m, l8)
            max_cur = max_nxt
        return m, l8, max_cur

    m0 = jnp.full((1, 2 * tq), -jnp.inf, F32)
    _, l8, _ = lax.fori_loop(0, nk // per_trip, body, (m0, jnp.zeros((8, 2 * tq), F32), scores(0, sta_ref)))
    lam = (jnp.exp(jnp.sum(lq1_ref[...] * lk1_ref[...], axis=-1, keepdims=True))
           - jnp.exp(jnp.sum(lq2_ref[...] * lk2_ref[...], axis=-1, keepdims=True)) + lam_init)
    o = acc_ref[...] / jnp.sum(l8, axis=0, keepdims=True)
    o = o[:, :tq] - lam * o[:, tq:]
    o = o * lax.rsqrt(jnp.mean(o * o, axis=0, keepdims=True) + EPS) * sg_ref[...]
    o_ref[...] = (o * (1.0 - lam_init)).T.astype(o_ref.dtype)


def flash_diff_attention(q, k, vt, lam_q1, lam_k1, lam_q2, lam_k2, subln_g, n_batch, lam_init):
    t, width = q.shape
    seq = t // n_batch
    heads = width // LANES
    tk = vt.shape[-1]
    tq = _row_tile(seq, 256)
    nq, nk = seq // tq, seq // tk
    kern = functools.partial(_flash_kernel, lam_init=lam_init)
    vec = lambda n: pl.BlockSpec((1, n), lambda b, h, qi: (0, 0))
    lam_vec = lambda x: x.astype(F32).reshape(1, ATT_HEAD_DIM)
    return pl.pallas_call(
        kern,
        out_shape=jax.ShapeDtypeStruct((t, width), BF16),
        grid=(n_batch, heads, nq),
        in_specs=[pl.BlockSpec((tq, LANES), lambda b, h, qi: (b * nq + qi, h)),
                  pl.BlockSpec((seq, LANES), lambda b, h, qi: (b, h)),
                  pl.BlockSpec((nk, 1, LANES, tk), lambda b, h, qi: (b, h, 0, 0)),
                  vec(ATT_HEAD_DIM), vec(ATT_HEAD_DIM), vec(ATT_HEAD_DIM), vec(ATT_HEAD_DIM),
                  pl.BlockSpec((LANES, 1), lambda b, h, qi: (0, 0))],
        out_specs=pl.BlockSpec((tq, LANES), lambda b, h, qi: (b * nq + qi, h)),
        scratch_shapes=[pltpu.VMEM((LANES, 2 * tq), F32), pltpu.VMEM((tk, 2 * tq), F32),
                        pltpu.VMEM((tk, 2 * tq), F32)],
        compiler_params=_params("parallel", "parallel", "arbitrary"),
        name="flash_diff_attention",
    )(q, k, vt, lam_vec(lam_q1), lam_vec(lam_k1), lam_vec(lam_q2), lam_vec(lam_k2),
      subln_g.astype(F32).reshape(LANES, 1))


def _proj_residual_kernel(x_ref, w_ref, h_ref, o_ref, o2_ref):
    out = h_ref[...] + jnp.dot(x_ref[...], w_ref[...], preferred_element_type=F32)
    o_ref[...] = out
    o2_ref[...] = out


def proj_residual(x, w, h):
    t, d = h.shape
    kdim = x.shape[1]
    tm = _row_tile(t, 512)
    row = lambda n: pl.BlockSpec((tm, n), lambda i: (i, 0))
    out = jax.ShapeDtypeStruct((t, d), F32)
    return pl.pallas_call(
        _proj_residual_kernel,
        out_shape=(out, out),
        grid=(t // tm,),
        in_specs=[row(kdim), pl.BlockSpec((kdim, d), lambda i: (0, 0)), row(d)],
        out_specs=(row(d), row(d)),
        compiler_params=_params("parallel"),
        name="proj_residual",
    )(x, w, h)


def _router_kernel(h_ref, g_ref, wt_hi_ref, wt_lo_ref, aff_ref):
    y = _rmsnorm_rows(h_ref[...], g_ref[...])
    y_hi, y_lo = _split_bf16(y)
    nt = (((1,), (1,)), ((), ()))
    logits = (lax.dot_general(wt_hi_ref[...], y_hi, nt, preferred_element_type=F32)
              + lax.dot_general(wt_hi_ref[...], y_lo, nt, preferred_element_type=F32)
              + lax.dot_general(wt_lo_ref[...], y_hi, nt, preferred_element_type=F32))
    e = jnp.exp(logits - jnp.max(logits, axis=0, keepdims=True))
    aff_ref[0] = e / jnp.sum(e, axis=0, keepdims=True)


def router(h, g, w_router, n_batch):
    t, d = h.shape
    seq = t // n_batch
    n_exp = w_router.shape[1]
    tm = _row_tile(seq, 512)
    ns = seq // tm
    wt = w_router.astype(F32).T
    wt_hi, wt_lo = _split_bf16(wt)
    return pl.pallas_call(
        _router_kernel,
        out_shape=jax.ShapeDtypeStruct((n_batch, n_exp, seq), F32),
        grid=(n_batch, ns),
        in_specs=[pl.BlockSpec((tm, d), lambda b, i: (b * ns + i, 0)),
                  pl.BlockSpec((1, d), lambda b, i: (0, 0)),
                  pl.BlockSpec((n_exp, d), lambda b, i: (0, 0)),
                  pl.BlockSpec((n_exp, d), lambda b, i: (0, 0))],
        out_specs=pl.BlockSpec((1, n_exp, tm), lambda b, i: (b, 0, i)),
        compiler_params=_params("parallel", "parallel"),
        name="router",
    )(h, g.reshape(1, d), wt_hi, wt_lo)


def _select_kernel(aff_ref, affw_ref, idx_ref, gate_ref, *, cap):
    rows_all = aff_ref.shape[1]
    n_exp = idx_ref.shape[1]
    r = rows_all // n_exp
    aff = aff_ref[0]
    bits_wide = pltpu.bitcast(affw_ref[0], jnp.int32)

    def count(mask):
        return jnp.sum(jnp.where(mask, 1.0, 0.0), axis=1, keepdims=True)

    def bit_step(i, thr):
        cand = thr | jnp.left_shift(jnp.int32(1), 30 - i)
        return jnp.where(count(bits_wide >= cand) >= cap, cand, thr)

    thr = lax.fori_loop(0, 31, bit_step, jnp.zeros((n_exp, 1), jnp.int32))
    n_tie_take = cap - count(bits_wide > thr)

    tri_incl = (lax.broadcasted_iota(jnp.int32, (LANES, LANES), 0)
                <= lax.broadcasted_iota(jnp.int32, (LANES, LANES), 1)).astype(BF16)
    ones_mat = jnp.ones((LANES, LANES), BF16)
    low_strict = (lax.broadcasted_iota(jnp.int32, (r, r), 1)
                  < lax.broadcasted_iota(jnp.int32, (r, r), 0)).astype(BF16)

    def prefix(mask2d):
        m = jnp.where(mask2d, 1.0, 0.0).astype(BF16)
        lane_incl = jnp.dot(m, tri_incl, preferred_element_type=F32)
        row_tot = jnp.dot(m, ones_mat, preferred_element_type=F32)
        row_off = jnp.dot(low_strict, row_tot.astype(BF16), preferred_element_type=F32)
        return lane_incl, row_tot, row_off

    slot = lax.broadcasted_iota(jnp.int32, (1, cap), 1).astype(F32)
    row_id = lax.broadcasted_iota(jnp.int32, (r, cap), 0).astype(F32)
    lane_id = lax.broadcasted_iota(jnp.int32, (LANES, cap), 0).astype(F32)
    tn = (((0,), (0,)), ((), ()))
    for e in range(n_exp):
        a_e = aff[e * r:(e + 1) * r, :]
        bits = pltpu.bitcast(a_e, jnp.int32)
        thr_e = thr[e:e + 1, :]
        tie_e = bits == thr_e
        t_incl, _, t_off = prefix(tie_e)
        tie_rank = t_off + t_incl - 1.0
        take = jnp.where(tie_rank < n_tie_take[e:e + 1, :], 1.0, 0.0) * jnp.where(tie_e, 1.0, 0.0)
        sel = jnp.where(bits > thr_e, 1.0, take) > 0.5
        lane_incl, row_tot, row_off = prefix(sel)
        off_b = jnp.tile(row_off, (1, cap // LANES))
        cum_b = jnp.tile(row_off + row_tot, (1, cap // LANES))
        row_of = jnp.sum(jnp.where(cum_b <= slot, 1.0, 0.0), axis=0, keepdims=True)
        off_of = jnp.max(jnp.where(off_b <= slot, off_b, 0.0), axis=0, keepdims=True)
        local = slot - off_of
        onehot = jnp.where(row_id == row_of, 1.0, 0.0).astype(BF16)
        incl_t = lax.dot_general(lane_incl.astype(BF16), onehot, tn, preferred_element_type=F32)
        lane_of = jnp.sum(jnp.where(incl_t <= local, 1.0, 0.0), axis=0, keepdims=True)
        idx_ref[0, e:e + 1, :] = (row_of * LANES + lane_of).astype(jnp.int32)
        a_hi = a_e.astype(BF16)
        a_mid = (a_e - a_hi.astype(F32)).astype(BF16)
        a_lo = (a_e - a_hi.astype(F32) - a_mid.astype(F32)).astype(BF16)
        a_t = (lax.dot_general(a_hi, onehot, tn, preferred_element_type=F32)
               + lax.dot_general(a_mid, onehot, tn, preferred_element_type=F32)
               + lax.dot_general(a_lo, onehot, tn, preferred_element_type=F32))
        gate_ref[0, e:e + 1, :] = jnp.sum(jnp.where(lane_id == lane_of, a_t, 0.0), axis=0, keepdims=True)


def expert_select(aff_t, cap):
    n_batch, n_exp, seq = aff_t.shape
    rows = n_exp * seq // LANES
    kern = functools.partial(_select_kernel, cap=cap)
    return pl.pallas_call(
        kern,
        out_shape=(jax.ShapeDtypeStruct((n_batch, n_exp, cap), jnp.int32),
                   jax.ShapeDtypeStruct((n_batch, n_exp, cap), F32)),
        grid=(n_batch,),
        in_specs=[pl.BlockSpec((1, rows, LANES), lambda b: (b, 0, 0)),
                  pl.BlockSpec((1, n_exp, seq), lambda b: (b, 0, 0))],
        out_specs=(pl.BlockSpec((1, n_exp, cap), lambda b: (b, 0, 0)),
                   pl.BlockSpec((1, n_exp, cap), lambda b: (b, 0, 0))),
        compiler_params=_params("parallel"),
        name="expert_select",
    )(aff_t.reshape(n_batch, rows, LANES), aff_t)


DMA_UNROLL = 8


def _cast_kernel(x_ref, o_ref):
    o_ref[...] = x_ref[...].astype(o_ref.dtype)


def cast_layer_bf16(w, layer):
    _, n, r, c = w.shape
    return pl.pallas_call(
        _cast_kernel,
        out_shape=jax.ShapeDtypeStruct((n, r, c), BF16),
        grid=(n,),
        in_specs=[pl.BlockSpec((None, 1, r, c), lambda i: (layer, i, 0, 0))],
        out_specs=pl.BlockSpec((1, r, c), lambda i: (i, 0, 0)),
        compiler_params=_params("parallel"),
        name="cast_bf16",
    )(w)


def _moe_kernel(idx_ref, gate_ref, g_ref, wg_ref, wu_ref, wd_ref, hin_ref, acc_in_ref, acc_ref,
                x_buf, o_buf, sem_x, sem_o, sem_s, *, seq, cap, n_batch):
    del acc_in_ref
    e, b, ti = pl.program_id(0), pl.program_id(1), pl.program_id(2)
    n_exp, nb, nt = pl.num_programs(0), pl.num_programs(1), pl.num_programs(2)
    tm = x_buf.shape[1]
    step = (e * nb + b) * nt + ti
    last = n_exp * nb * nt - 1
    slot = step % 2
    other = 1 - slot
    nxt = jnp.minimum(step + 1, last)
    e_n, b_n, ti_n = nxt // (nb * nt), (nxt // nt) % nb, nxt % nt
    base = (b * n_exp + e) * cap + ti * tm
    base_n = (b_n * n_exp + e_n) * cap + ti_n * tm

    def x_copy(j, src_base, src_batch, dst_slot):
        row = pl.ds(src_batch * seq + idx_ref[src_base + j], 1)
        return pltpu.make_async_copy(hin_ref.at[row], x_buf.at[dst_slot, pl.ds(j, 1)], sem_x.at[dst_slot])

    def acc_rows(j):
        return acc_ref.at[pl.ds(b * seq + idx_ref[base + j], 1)]

    def wait_x(s):
        pltpu.make_async_copy(hin_ref.at[pl.ds(0, tm)], x_buf.at[s], sem_x.at[s]).wait()

    def wait_scatter(s):
        pltpu.make_async_copy(o_buf.at[s], acc_ref.at[pl.ds(0, tm)], sem_s.at[s]).wait()

    @pl.when(step == 0)
    def _():
        def first(jj, c):
            for u in range(DMA_UNROLL):
                x_copy(jj * DMA_UNROLL + u, base, b, slot).start()
            return c
        lax.fori_loop(0, tm // DMA_UNROLL, first, 0)

    if n_batch == 1:
        @pl.when(step > 0)
        def _():
            wait_scatter(other)

    wait_x(slot)
    for j in range(tm):
        x_copy(j, base_n, b_n, other).start()
        pltpu.make_async_copy(acc_rows(j), o_buf.at[slot, pl.ds(j, 1)], sem_o.at[slot]).start()
    x = _rmsnorm_rows(x_buf[slot], g_ref[...]).astype(BF16)
    a = jnp.dot(x, wg_ref[0], preferred_element_type=F32)
    u = jnp.dot(x, wu_ref[0], preferred_element_type=F32)
    hmid = (a * (1.0 / (1.0 + jnp.exp(-a))) * u).astype(BF16)
    y = jnp.dot(hmid, wd_ref[0], preferred_element_type=F32)
    pltpu.make_async_copy(acc_ref.at[pl.ds(0, tm)], o_buf.at[slot], sem_o.at[slot]).wait()
    o_buf[slot] = o_buf[slot] + y * gate_ref[...]

    def start_scatter(jj, c):
        for u in range(DMA_UNROLL):
            j = jj * DMA_UNROLL + u
            pltpu.make_async_copy(o_buf.at[slot, pl.ds(j, 1)], acc_rows(j), sem_s.at[slot]).start()
        return c

    lax.fori_loop(0, tm // DMA_UNROLL, start_scatter, 0)

    if n_batch > 1:
        @pl.when(step > 0)
        def _():
            wait_scatter(other)

    @pl.when(step == last)
    def _():
        wait_x(other)
        wait_scatter(slot)


def moe_ffn(h, acc, g, idx, gate, w_gate, w_up, w_down, n_batch):
    t, d = h.shape
    seq = t // n_batch
    n_exp, _, ff = w_gate.shape
    cap = idx.shape[-1]
    tm = _row_tile(cap, 512)
    nt = cap // tm
    kern = functools.partial(_moe_kernel, seq=seq, cap=cap, n_batch=n_batch)
    grid_spec = pltpu.PrefetchScalarGridSpec(
        num_scalar_prefetch=1,
        grid=(n_exp, n_batch, nt),
        in_specs=[pl.BlockSpec((tm, 1), lambda e, b, i, idx: ((b * n_exp + e) * nt + i, 0)),
                  pl.BlockSpec((1, d), lambda e, b, i, idx: (0, 0)),
                  pl.BlockSpec((1, d, ff), lambda e, b, i, idx: (e, 0, 0)),
                  pl.BlockSpec((1, d, ff), lambda e, b, i, idx: (e, 0, 0)),
                  pl.BlockSpec((1, ff, d), lambda e, b, i, idx: (e, 0, 0)),
                  pl.BlockSpec(memory_space=pl.ANY),
                  pl.BlockSpec(memory_space=pl.ANY)],
        out_specs=pl.BlockSpec(memory_space=pl.ANY),
        scratch_shapes=[pltpu.VMEM((2, tm, d), F32), pltpu.VMEM((2, tm, d), F32), pltpu.SemaphoreType.DMA((2,)),
                        pltpu.SemaphoreType.DMA((2,)), pltpu.SemaphoreType.DMA((2,))],
    )
    return pl.pallas_call(
        kern,
        out_shape=jax.ShapeDtypeStruct((t, d), F32),
        grid_spec=grid_spec,
        input_output_aliases={7: 0},
        compiler_params=_params("arbitrary", "arbitrary", "arbitrary"),
        name="moe_ffn",
    )(idx.reshape(-1), gate.reshape(-1, 1), g.reshape(1, d), w_gate, w_up, w_down, h, acc)


def _even_layer(h, n_batch, seq, g_mix, w_in, w_out, s5, d_skip, w_glu, b_glu, pool_w, pool_scale):
    t = h.shape[0]
    z = norm_proj(h, g_mix, w_in.astype(BF16), F32)
    half = z.shape[1] // 2
    n_grp = half // S5_GROUP
    nc = t // S5_CHUNK
    u = z[:, :half].astype(BF16).reshape(nc, S5_CHUNK, n_grp, S5_GROUP)
    u = u.transpose(2, 0, 1, 3).reshape(n_grp, nc, S5_CHUNK * S5_GROUP)
    wt, wst, wout, al = s5_chunk_matrices(*s5)
    ys = s5_scan(u, wt.astype(BF16), wst.astype(BF16), wout.astype(BF16), al, n_batch)
    ys = ys.reshape(n_grp, nc, S5_CHUNK, S5_GROUP).transpose(1, 2, 0, 3).reshape(t, half)
    return even_post(z, ys, d_skip.astype(F32), w_glu.astype(BF16), b_glu.astype(F32), pool_w.astype(BF16),
                     pool_scale.astype(F32), w_out.astype(BF16), h, seq)


def _odd_layer(h, n_batch, layer, g_mix, cos, sin, w_qkv, w_out, q_gain, k_gain, lq1, lk1, lq2, lk2, subln_g):
    q, k, vt = attn_qkv(h, g_mix, w_qkv, cos, sin, q_gain, k_gain)
    lam_init = 0.8 - 0.6 * math.exp(-0.3 * layer)
    o = flash_diff_attention(q, k, vt, lq1, lk1, lq2, lk2, subln_g, n_batch, lam_init)
    return proj_residual(o, w_out.astype(BF16), h)


def _moe_layer(h, acc, n_batch, layer, g_ffn, w_router, w_gate_all, w_up_all, w_down_all):
    seq = h.shape[0] // n_batch
    cap = CAPACITY_FACTOR * seq // N_EXPERTS
    aff_t = router(h, g_ffn, w_router, n_batch)
    idx, gate = expert_select(aff_t, cap)
    return moe_ffn(h, acc, g_ffn, idx, gate, cast_layer_bf16(w_gate_all, layer), cast_layer_bf16(w_up_all, layer),
                   cast_layer_bf16(w_down_all, layer), n_batch)


def kernel(x, positions, norm_mix_g, norm_ffn_g, hyb_w_in, hyb_w_out, s5_lam_re, s5_lam_im, s5_log_dt, s5_b_re, s5_b_im, s5_c_re, s5_c_im, s5_d, s5_w_glu, s5_b_glu, pool_w, pool_scale, attn_w_qkv, attn_w_out, attn_q_norm_g, attn_k_norm_g, attn_lam_q1, attn_lam_k1, attn_lam_q2, attn_lam_k2, attn_subln_g, moe_w_router, moe_w_gate, moe_w_up, moe_w_down):
    n_batch, seq, d = x.shape
    depth = norm_mix_g.shape[0]
    h = x.reshape(n_batch * seq, d)
    cos, sin = rope_tables(positions)
    for layer in range(depth):
        if layer % 2 == 0:
            e = layer // 2
            s5 = (s5_lam_re[e], s5_lam_im[e], s5_log_dt[e], s5_b_re[e], s5_b_im[e], s5_c_re[e], s5_c_im[e])
            h, acc = _even_layer(h, n_batch, seq, norm_mix_g[layer], hyb_w_in[e], hyb_w_out[e], s5, s5_d[e],
                                 s5_w_glu[e], s5_b_glu[e], pool_w[e], pool_scale[e])
        else:
            o = layer // 2
            h, acc = _odd_layer(h, n_batch, layer, norm_mix_g[layer], cos, sin, attn_w_qkv[o], attn_w_out[o],
                                attn_q_norm_g[o], attn_k_norm_g[o], attn_lam_q1[o], attn_lam_k1[o],
                                attn_lam_q2[o], attn_lam_k2[o], attn_subln_g[o])
        h = _moe_layer(h, acc, n_batch, layer, norm_ffn_g[layer], moe_w_router[layer], moe_w_gate,
                       moe_w_up, moe_w_down)
    return h.reshape(n_batch, seq, d)
```

```python
import functools
import math

import jax
import jax.numpy as jnp
from jax import lax
from jax.experimental import pallas as pl
from jax.experimental.pallas import tpu as pltpu

F32 = jnp.float32
BF16 = jnp.bfloat16

EPS = 1e-6
LANES = 128
VMEM_LIMIT = 56 * 1024 * 1024

S5_GROUP = 16
S5_CHUNK = 16
POOL_WINDOWS = (2, 4, 8, 16)
POOL_HALO = 8
ATT_HEAD_DIM = 64
ROPE_THETA = 10000.0
N_EXPERTS = 16
CAPACITY_FACTOR = 2


def _row_tile(n, want):
    t = min(n, want)
    assert n % t == 0, (n, t)
    return t


def _params(*sem):
    return pltpu.CompilerParams(dimension_semantics=sem, vmem_limit_bytes=VMEM_LIMIT)


def _rmsnorm_rows(x, g):
    return x * lax.rsqrt(jnp.mean(x * x, axis=-1, keepdims=True) + EPS) * g


def _split_bf16(x):
    hi = x.astype(BF16)
    lo = (x - hi.astype(F32)).astype(BF16)
    return hi, lo


def _norm_proj_kernel(h_ref, g_ref, w_ref, o_ref):
    y = _rmsnorm_rows(h_ref[...], g_ref[...])
    o_ref[...] = jnp.dot(y.astype(BF16), w_ref[...], preferred_element_type=F32).astype(o_ref.dtype)


def norm_proj(h, g, w, out_dtype):
    t, d = h.shape
    n = w.shape[1]
    tm = _row_tile(t, 512)
    return pl.pallas_call(
        _norm_proj_kernel,
        out_shape=jax.ShapeDtypeStruct((t, n), out_dtype),
        grid=(t // tm,),
        in_specs=[pl.BlockSpec((tm, d), lambda i: (i, 0)),
                  pl.BlockSpec((1, d), lambda i: (0, 0)),
                  pl.BlockSpec((d, n), lambda i: (0, 0))],
        out_specs=pl.BlockSpec((tm, n), lambda i: (i, 0)),
        compiler_params=_params("parallel"),
        name="norm_proj",
    )(h, g.reshape(1, d), w)


def _s5_kernel(u_ref, wt_ref, wst_ref, wout_ref, al_ref, y_ref, s_ref, hf_ref, hb_ref, *, n_batch):
    nc_all = u_ref.shape[1]
    nc = nc_all // n_batch
    u = u_ref[0]
    s_ref[...] = jnp.dot(u, wst_ref[0], preferred_element_type=F32)
    al = al_ref[0]
    a_re = al[0:1, :]
    a_im = al[1:2, :]
    is_fwd = lax.broadcasted_iota(jnp.int32, (1, 2 * LANES), 1) % LANES < LANES // 2

    def step(i, carry):
        new = []
        for b in range(n_batch):
            h = carry[b]
            row_f = b * nc + i
            row_b = b * nc + nc - 1 - i
            hf_ref[pl.ds(row_f, 1), :] = h
            hb_ref[pl.ds(row_b, 1), :] = h
            s = jnp.where(is_fwd, s_ref[pl.ds(row_f, 1), :], s_ref[pl.ds(row_b, 1), :])
            h_re = h[:, :LANES]
            h_im = h[:, LANES:]
            n_re = a_re * h_re - a_im * h_im + s[:, :LANES]
            n_im = a_re * h_im + a_im * h_re + s[:, LANES:]
            new.append(jnp.concatenate([n_re, n_im], axis=1))
        return tuple(new)

    zero = jnp.zeros((1, 2 * LANES), F32)
    lax.fori_loop(0, nc, step, tuple(zero for _ in range(n_batch)))
    h_in = jnp.where(is_fwd, hf_ref[...], hb_ref[...])
    y = jnp.dot(u, wt_ref[0], preferred_element_type=F32)
    y = y + jnp.dot(h_in.astype(BF16), wout_ref[0], preferred_element_type=F32)
    y_ref[0] = y.astype(y_ref.dtype)


def s5_scan(u, wt, wst, wout, al, n_batch):
    g, nc_all, w = u.shape
    kern = functools.partial(_s5_kernel, n_batch=n_batch)
    mat = pl.BlockSpec((1, w, w), lambda i: (i, 0, 0))
    return pl.pallas_call(
        kern,
        out_shape=jax.ShapeDtypeStruct((g, nc_all, w), BF16),
        grid=(g,),
        in_specs=[pl.BlockSpec((1, nc_all, w), lambda i: (i, 0, 0)), mat, mat, mat,
                  pl.BlockSpec((1, 2, LANES), lambda i: (i, 0, 0))],
        out_specs=pl.BlockSpec((1, nc_all, w), lambda i: (i, 0, 0)),
        scratch_shapes=[pltpu.VMEM((nc_all, w), F32), pltpu.VMEM((nc_all, w), F32),
                        pltpu.VMEM((nc_all, w), F32)],
        compiler_params=_params("parallel"),
        name="s5_scan",
    )(u, wt, wst, wout, al)


def s5_chunk_matrices(lam_re, lam_im, log_dt, b_re, b_im, c_re, c_im):
    L = S5_CHUNK
    lam = lax.complex(lam_re.astype(F32), lam_im.astype(F32))
    dt = jnp.exp(log_dt.astype(F32))[..., None]
    lam_dt = lam * dt
    a_bar = jnp.exp(lam_dt)
    b = lax.complex(b_re.astype(F32), b_im.astype(F32))
    b_bar = ((a_bar - 1.0) / lam)[..., None] * b
    c = lax.complex(c_re.astype(F32), c_im.astype(F32))
    k = jnp.arange(L + 1, dtype=F32)
    pw = jnp.exp(lam_dt[:, :, None, :] * k[None, None, :, None])
    n_grp, n_state = lam.shape[1], lam.shape[2]
    kern = jnp.einsum('dgop,dgkp,dgpi->dgkoi', c, pw[:, :, :L], b_bar).real
    t_idx = jnp.arange(L)[None, :]
    s_idx = jnp.arange(L)[:, None]
    lag_f = t_idx - s_idx
    kf = jnp.where((lag_f >= 0)[None, :, :, None, None], kern[0][:, jnp.clip(lag_f, 0, L - 1)], 0.0)
    kb = jnp.where((lag_f <= 0)[None, :, :, None, None], kern[1][:, jnp.clip(-lag_f, 0, L - 1)], 0.0)
    wt = (kf + kb).transpose(0, 1, 4, 2, 3).reshape(n_grp, L * S5_GROUP, L * S5_GROUP)
    cf = pw[0][:, L - 1 - jnp.arange(L)][..., None] * b_bar[0][:, None]
    cb = pw[1][:, jnp.arange(L)][..., None] * b_bar[1][:, None]
    def st(x):
        return x.transpose(0, 1, 3, 2).reshape(n_grp, L * S5_GROUP, n_state)
    wst = jnp.concatenate([st(cf.real), st(cb.real), st(cf.imag), st(cb.imag)], axis=-1)
    of = c[0][:, None] * pw[0][:, 1 + jnp.arange(L)][:, :, None, :]
    ob = c[1][:, None] * pw[1][:, L - jnp.arange(L)][:, :, None, :]
    def ot(x):
        return x.transpose(0, 3, 1, 2).reshape(n_grp, n_state, L * S5_GROUP)
    wout = jnp.concatenate([ot(of.real), ot(ob.real), ot(-of.imag), ot(-ob.imag)], axis=1)
    a_l = pw[:, :, L]
    al = jnp.stack([jnp.concatenate([a_l[0].real, a_l[1].real], axis=-1),
                    jnp.concatenate([a_l[0].imag, a_l[1].imag], axis=-1)], axis=1)
    return wt, wst, wout, al


def _gelu_tanh(x):
    return 0.5 * x * (1.0 + jnp.tanh(math.sqrt(2.0 / math.pi) * (x + 0.044715 * (x * x * x))))


def _even_post_kernel(z_ref, zp_ref, zn_ref, ys_ref, d_ref, wglu_ref, bglu_ref, pw_ref, ps_ref, wout_ref,
                      h_ref, o_ref, o2_ref, ext_ref, cat_ref, *, seq):
    tm = z_ref.shape[0]
    half = z_ref.shape[1] // 2
    t0 = (pl.program_id(0) * tm) % seq
    y = d_ref[...] * z_ref[:, :half] + ys_ref[...].astype(F32)
    y = _gelu_tanh(y)
    gate = jnp.dot(y.astype(BF16), wglu_ref[...], preferred_element_type=F32) + bglu_ref[...]
    cat_ref[:, :half] = (y * (1.0 / (1.0 + jnp.exp(-gate)))).astype(BF16)
    v = z_ref[:, half:]
    ext_ref[0:POOL_HALO, :] = jnp.where(t0 > 0, zp_ref[...], 0.0)
    ext_ref[POOL_HALO:POOL_HALO + tm, :] = v
    ext_ref[POOL_HALO + tm:, :] = jnp.where(t0 + tm < seq, zn_ref[...], 0.0)
    pos = t0 + lax.broadcasted_iota(jnp.int32, (tm, 1), 0)
    grp = half // len(POOL_WINDOWS)
    for gi, win in enumerate(POOL_WINDOWS):
        cols = slice(gi * grp, (gi + 1) * grp)
        acc = ext_ref[POOL_HALO - win // 2:POOL_HALO - win // 2 + tm, cols]
        for j in range(1, win):
            off = POOL_HALO - win // 2 + j
            acc = acc + ext_ref[off:off + tm, cols]
        cnt = jnp.minimum(pos + win // 2, seq) - jnp.maximum(pos - win // 2, 0)
        pooled = acc / cnt.astype(F32) - v[:, cols]
        yb = jnp.dot(pooled.astype(BF16), pw_ref[gi], preferred_element_type=F32) * ps_ref[:, cols]
        cat_ref[:, half + gi * grp:half + (gi + 1) * grp] = yb.astype(BF16)
    out = h_ref[...] + jnp.dot(cat_ref[...], wout_ref[...], preferred_element_type=F32)
    o_ref[...] = out
    o2_ref[...] = out


def even_post(z, ys, d_skip, w_glu, b_glu, pool_w, pool_scale, w_out, h, seq):
    t, width = z.shape
    half = width // 2
    d = h.shape[1]
    tm = _row_tile(seq, 512)
    hb = tm // POOL_HALO
    nblk8 = t // POOL_HALO
    kern = functools.partial(_even_post_kernel, seq=seq)
    full = lambda shape: pl.BlockSpec(shape, lambda i: (0,) * len(shape))
    return pl.pallas_call(
        kern,
        out_shape=(jax.ShapeDtypeStruct((t, d), F32), jax.ShapeDtypeStruct((t, d), F32)),
        grid=(t // tm,),
        in_specs=[pl.BlockSpec((tm, width), lambda i: (i, 0)),
                  pl.BlockSpec((POOL_HALO, half), lambda i: (jnp.maximum(i * hb - 1, 0), 1)),
                  pl.BlockSpec((POOL_HALO, half), lambda i: (jnp.minimum((i + 1) * hb, nblk8 - 1), 1)),
                  pl.BlockSpec((tm, half), lambda i: (i, 0)),
                  full((1, half)), full((half, half)), full((1, half)),
                  full(pool_w.shape), full((1, half)), full((width, d)),
                  pl.BlockSpec((tm, d), lambda i: (i, 0))],
        out_specs=(pl.BlockSpec((tm, d), lambda i: (i, 0)), pl.BlockSpec((tm, d), lambda i: (i, 0))),
        scratch_shapes=[pltpu.VMEM((tm + 2 * POOL_HALO, half), F32), pltpu.VMEM((tm, width), BF16)],
        compiler_params=_params("parallel"),
        name="even_post",
    )(z, z, z, ys, d_skip.reshape(1, half), w_glu, b_glu.reshape(1, half), pool_w,
      pool_scale.reshape(1, half), w_out, h)


def _rope_kernel(pos_ref, inv_ref, sign_ref, cos_ref, sin_ref):
    ang = pos_ref[...].astype(F32) * inv_ref[...]
    cos_ref[...] = jnp.cos(ang)
    sin_ref[...] = jnp.sin(ang) * sign_ref[...]


def rope_tables(positions):
    t = positions.size
    half = ATT_HEAD_DIM // 2
    inv = ROPE_THETA ** (-jnp.arange(0, ATT_HEAD_DIM, 2, dtype=F32) / ATT_HEAD_DIM)
    reps = LANES // half
    inv_pat = jnp.tile(inv, reps).reshape(1, LANES)
    sign = jnp.tile(jnp.concatenate([-jnp.ones((half,), F32), jnp.ones((half,), F32)]), reps // 2).reshape(1, LANES)
    pos = jnp.broadcast_to(positions.reshape(t, 1), (t, LANES))
    tm = _row_tile(t, 1024)
    return pl.pallas_call(
        _rope_kernel,
        out_shape=(jax.ShapeDtypeStruct((t, LANES), F32), jax.ShapeDtypeStruct((t, LANES), F32)),
        grid=(t // tm,),
        in_specs=[pl.BlockSpec((tm, LANES), lambda i: (i, 0)),
                  pl.BlockSpec((1, LANES), lambda i: (0, 0)),
                  pl.BlockSpec((1, LANES), lambda i: (0, 0))],
        out_specs=(pl.BlockSpec((tm, LANES), lambda i: (i, 0)), pl.BlockSpec((tm, LANES), lambda i: (i, 0))),
        compiler_params=_params("parallel"),
        name="rope_tables",
    )(pos, inv_pat, sign)


ATT_KV_CHUNK = 512


def _qkv_kernel(h_ref, g_ref, wqk_ref, wvt_ref, cos_ref, sin_ref, qg_ref, kg_ref, seg_ref, q_ref, k_ref, vt_ref):
    d = h_ref.shape[1]
    y = _rmsnorm_rows(h_ref[...], g_ref[...]).astype(BF16)
    cos = cos_ref[...]
    sin = sin_ref[...]
    seg = seg_ref[...]
    first_half = lax.broadcasted_iota(jnp.int32, (1, LANES), 1) % ATT_HEAD_DIM < ATT_HEAD_DIM // 2
    q_scale = ATT_HEAD_DIM ** -0.5 * math.log2(math.e)
    for which, (o_ref, gain_ref, scale) in enumerate(((q_ref, qg_ref, q_scale), (k_ref, kg_ref, 1.0))):
        x_all = jnp.dot(y, wqk_ref[:, which * d:(which + 1) * d], preferred_element_type=F32)
        for j in range(d // LANES):
            x = x_all[:, j * LANES:(j + 1) * LANES]
            hi, lo = _split_bf16(x * x)
            ss = jnp.dot(hi, seg, preferred_element_type=F32) + jnp.dot(lo, seg, preferred_element_type=F32)
            xn = x * lax.rsqrt(ss * (1.0 / ATT_HEAD_DIM) + EPS) * gain_ref[...]
            swapped = jnp.where(first_half, pltpu.roll(xn, LANES - ATT_HEAD_DIM // 2, 1),
                                pltpu.roll(xn, ATT_HEAD_DIM // 2, 1))
            o_ref[:, j * LANES:(j + 1) * LANES] = ((xn * cos + swapped * sin) * scale).astype(BF16)
    vt = lax.dot_general(wvt_ref[...], y, (((1,), (1,)), ((), ())), preferred_element_type=F32)
    vt_ref[0] = vt.reshape(vt_ref.shape[1:]).astype(BF16)


def attn_qkv(h, g, w_qkv, cos, sin, q_gain, k_gain):
    t, d = h.shape
    tm = _row_tile(t, ATT_KV_CHUNK)
    heads = d // LANES
    reps = LANES // ATT_HEAD_DIM
    lane = jnp.arange(LANES)
    seg = (lane[:, None] // ATT_HEAD_DIM == lane[None, :] // ATT_HEAD_DIM).astype(BF16)
    w_qk = w_qkv[:, :2 * d].astype(BF16)
    w_vt = w_qkv[:, 2 * d:].T.astype(BF16)
    row = lambda n: pl.BlockSpec((tm, n), lambda i: (i, 0))
    one = lambda n: pl.BlockSpec((1, n), lambda i: (0, 0))
    out = jax.ShapeDtypeStruct((t, d), BF16)
    return pl.pallas_call(
        _qkv_kernel,
        out_shape=(out, out, jax.ShapeDtypeStruct((t // tm, heads, LANES, tm), BF16)),
        grid=(t // tm,),
        in_specs=[row(d), one(d), pl.BlockSpec(w_qk.shape, lambda i: (0, 0)), pl.BlockSpec(w_vt.shape, lambda i: (0, 0)),
                  row(LANES), row(LANES), one(LANES), one(LANES), pl.BlockSpec((LANES, LANES), lambda i: (0, 0))],
        out_specs=(row(d), row(d), pl.BlockSpec((1, heads, LANES, tm), lambda i: (i, 0, 0, 0))),
        compiler_params=_params("parallel"),
        name="attn_qkv",
    )(h, g.reshape(1, d), w_qk, w_vt, cos, sin, jnp.tile(q_gain.astype(F32), reps).reshape(1, LANES),
      jnp.tile(k_gain.astype(F32), reps).reshape(1, LANES), seg)


def _flash_kernel(q_ref, k_ref, vt_ref, lq1_ref, lk1_ref, lq2_ref, lk2_ref, sg_ref, o_ref, acc_ref, sta_ref, stb_ref,
                  *, lam_init):
    tq = q_ref.shape[0]
    nk, _, _, tk = vt_ref.shape
    q = q_ref[...]
    first = lax.broadcasted_iota(jnp.int32, (1, LANES), 1) < ATT_HEAD_DIM
    zero = jnp.zeros_like(q)
    qm = jnp.concatenate([jnp.where(first, q, zero), jnp.where(first, zero, q)], axis=0)
    acc_ref[...] = jnp.zeros(acc_ref.shape, F32)
    nt = (((1,), (1,)), ((), ()))

    def scores(i, st_ref):
        kc = k_ref[pl.ds(pl.multiple_of(i * tk, tk), tk), :]
        st = lax.dot_general(kc, qm, nt, preferred_element_type=F32)
        st_ref[...] = st
        return jnp.max(st, axis=0, keepdims=True)

    def absorb(i, st_ref, chunk_max, m_prev, l8):
        m_new = jnp.maximum(m_prev, chunk_max)
        alpha = jnp.exp2(m_prev - m_new)
        p = jnp.exp2(st_ref[...] - m_new)
        l8 = alpha * l8 + jnp.sum(p.reshape(tk // 8, 8, 2 * tq), axis=0)
        acc_ref[...] = alpha * acc_ref[...] + jnp.dot(vt_ref[i, 0], p.astype(BF16), preferred_element_type=F32)
        return m_new, l8

    per_trip = 16 if nk % 16 == 0 else (8 if nk % 8 == 0 else 2)
    assert nk % per_trip == 0

    def body(j, carry):
        m, l8, max_cur = carry
        bufs = (sta_ref, stb_ref)
        for u in range(per_trip):
            i = per_trip * j + u
            nxt = i + 1 if u + 1 < per_trip else jnp.minimum(i + 1, nk - 1)
            max_nxt = scores(nxt, bufs[(u + 1) % 2])
            m, l8 = absorb(i, bufs[u % 2], max_cur, m, l8)
            max_cur = max_nxt
        return m, l8, max_cur

    m0 = jnp.full((1, 2 * tq), -jnp.inf, F32)
    _, l8, _ = lax.fori_loop(0, nk // per_trip, body, (m0, jnp.zeros((8, 2 * tq), F32), scores(0, sta_ref)))
    lam = (jnp.exp(jnp.sum(lq1_ref[...] * lk1_ref[...], axis=-1, keepdims=True))
           - jnp.exp(jnp.sum(lq2_ref[...] * lk2_ref[...], axis=-1, keepdims=True)) + lam_init)
    o = acc_ref[...] / jnp.sum(l8, axis=0, keepdims=True)
    o = o[:, :tq] - lam * o[:, tq:]
    o = o * lax.rsqrt(jnp.mean(o * o, axis=0, keepdims=True) + EPS) * sg_ref[...]
    o_ref[...] = (o * (1.0 - lam_init)).T.astype(o_ref.dtype)


def flash_diff_attention(q, k, vt, lam_q1, lam_k1, lam_q2, lam_k2, subln_g, n_batch, lam_init):
    t, width = q.shape
    seq = t // n_batch
    heads = width // LANES
    tk = vt.shape[-1]
    tq = _row_tile(seq, 256)
    nq, nk = seq // tq, seq // tk
    kern = functools.partial(_flash_kernel, lam_init=lam_init)
    vec = lambda n: pl.BlockSpec((1, n), lambda b, h, qi: (0, 0))
    lam_vec = lambda x: x.astype(F32).reshape(1, ATT_HEAD_DIM)
    return pl.pallas_call(
        kern,
        out_shape=jax.ShapeDtypeStruct((t, width), BF16),
        grid=(n_batch, heads, nq),
        in_specs=[pl.BlockSpec((tq, LANES), lambda b, h, qi: (b * nq + qi, h)),
                  pl.BlockSpec((seq, LANES), lambda b, h, qi: (b, h)),
                  pl.BlockSpec((nk, 1, LANES, tk), lambda b, h, qi: (b, h, 0, 0)),
                  vec(ATT_HEAD_DIM), vec(ATT_HEAD_DIM), vec(ATT_HEAD_DIM), vec(ATT_HEAD_DIM),
                  pl.BlockSpec((LANES, 1), lambda b, h, qi: (0, 0))],
        out_specs=pl.BlockSpec((tq, LANES), lambda b, h, qi: (b * nq + qi, h)),
        scratch_shapes=[pltpu.VMEM((LANES, 2 * tq), F32), pltpu.VMEM((tk, 2 * tq), F32),
                        pltpu.VMEM((tk, 2 * tq), F32)],
        compiler_params=_params("parallel", "parallel", "arbitrary"),
        name="flash_diff_attention",
    )(q, k, vt, lam_vec(lam_q1), lam_vec(lam_k1), lam_vec(lam_q2), lam_vec(lam_k2),
      subln_g.astype(F32).reshape(LANES, 1))


def _proj_residual_kernel(x_ref, w_ref, h_ref, o_ref, o2_ref):
    out = h_ref[...] + jnp.dot(x_ref[...], w_ref[...], preferred_element_type=F32)
    o_ref[...] = out
    o2_ref[...] = out


def proj_residual(x, w, h):
    t, d = h.shape
    kdim = x.shape[1]
    tm = _row_tile(t, 512)
    row = lambda n: pl.BlockSpec((tm, n), lambda i: (i, 0))
    out = jax.ShapeDtypeStruct((t, d), F32)
    return pl.pallas_call(
        _proj_residual_kernel,
        out_shape=(out, out),
        grid=(t // tm,),
        in_specs=[row(kdim), pl.BlockSpec((kdim, d), lambda i: (0, 0)), row(d)],
        out_specs=(row(d), row(d)),
        compiler_params=_params("parallel"),
        name="proj_residual",
    )(x, w, h)


def _router_kernel(h_ref, g_ref, wt_hi_ref, wt_lo_ref, aff_ref):
    y = _rmsnorm_rows(h_ref[...], g_ref[...])
    y_hi, y_lo = _split_bf16(y)
    nt = (((1,), (1,)), ((), ()))
    logits = (lax.dot_general(wt_hi_ref[...], y_hi, nt, preferred_element_type=F32)
              + lax.dot_general(wt_hi_ref[...], y_lo, nt, preferred_element_type=F32)
              + lax.dot_general(wt_lo_ref[...], y_hi, nt, preferred_element_type=F32))
    e = jnp.exp(logits - jnp.max(logits, axis=0, keepdims=True))
    aff_ref[0] = e / jnp.sum(e, axis=0, keepdims=True)


def router(h, g, w_router, n_batch):
    t, d = h.shape
    seq = t // n_batch
    n_exp = w_router.shape[1]
    tm = _row_tile(seq, 512)
    ns = seq // tm
    wt = w_router.astype(F32).T
    wt_hi, wt_lo = _split_bf16(wt)
    return pl.pallas_call(
        _router_kernel,
        out_shape=jax.ShapeDtypeStruct((n_batch, n_exp, seq), F32),
        grid=(n_batch, ns),
        in_specs=[pl.BlockSpec((tm, d), lambda b, i: (b * ns + i, 0)),
                  pl.BlockSpec((1, d), lambda b, i: (0, 0)),
                  pl.BlockSpec((n_exp, d), lambda b, i: (0, 0)),
                  pl.BlockSpec((n_exp, d), lambda b, i: (0, 0))],
        out_specs=pl.BlockSpec((1, n_exp, tm), lambda b, i: (b, 0, i)),
        compiler_params=_params("parallel", "parallel"),
        name="router",
    )(h, g.reshape(1, d), wt_hi, wt_lo)


def _select_kernel(aff_ref, affw_ref, idx_ref, gate_ref, *, cap):
    rows_all = aff_ref.shape[1]
    n_exp = idx_ref.shape[1]
    r = rows_all // n_exp
    aff = aff_ref[0]
    bits_wide = pltpu.bitcast(affw_ref[0], jnp.int32)

    def count(mask):
        return jnp.sum(jnp.where(mask, 1.0, 0.0), axis=1, keepdims=True)

    def bit_step(i, thr):
        cand = thr | jnp.left_shift(jnp.int32(1), 30 - i)
        return jnp.where(count(bits_wide >= cand) >= cap, cand, thr)

    thr = lax.fori_loop(0, 31, bit_step, jnp.zeros((n_exp, 1), jnp.int32))
    n_tie_take = cap - count(bits_wide > thr)

    tri_incl = (lax.broadcasted_iota(jnp.int32, (LANES, LANES), 0)
                <= lax.broadcasted_iota(jnp.int32, (LANES, LANES), 1)).astype(BF16)
    ones_mat = jnp.ones((LANES, LANES), BF16)
    low_strict = (lax.broadcasted_iota(jnp.int32, (r, r), 1)
                  < lax.broadcasted_iota(jnp.int32, (r, r), 0)).astype(BF16)

    def prefix(mask2d):
        m = jnp.where(mask2d, 1.0, 0.0).astype(BF16)
        lane_incl = jnp.dot(m, tri_incl, preferred_element_type=F32)
        row_tot = jnp.dot(m, ones_mat, preferred_element_type=F32)
        row_off = jnp.dot(low_strict, row_tot.astype(BF16), preferred_element_type=F32)
        return lane_incl, row_tot, row_off

    slot = lax.broadcasted_iota(jnp.int32, (1, cap), 1).astype(F32)
    row_id = lax.broadcasted_iota(jnp.int32, (r, cap), 0).astype(F32)
    lane_id = lax.broadcasted_iota(jnp.int32, (LANES, cap), 0).astype(F32)
    tn = (((0,), (0,)), ((), ()))
    for e in range(n_exp):
        a_e = aff[e * r:(e + 1) * r, :]
        bits = pltpu.bitcast(a_e, jnp.int32)
        thr_e = thr[e:e + 1, :]
        tie_e = bits == thr_e
        t_incl, _, t_off = prefix(tie_e)
        tie_rank = t_off + t_incl - 1.0
        take = jnp.where(tie_rank < n_tie_take[e:e + 1, :], 1.0, 0.0) * jnp.where(tie_e, 1.0, 0.0)
        sel = jnp.where(bits > thr_e, 1.0, take) > 0.5
        lane_incl, row_tot, row_off = prefix(sel)
        off_b = jnp.tile(row_off, (1, cap // LANES))
        cum_b = jnp.tile(row_off + row_tot, (1, cap // LANES))
        row_of = jnp.sum(jnp.where(cum_b <= slot, 1.0, 0.0), axis=0, keepdims=True)
        off_of = jnp.max(jnp.where(off_b <= slot, off_b, 0.0), axis=0, keepdims=True)
        local = slot - off_of
        onehot = jnp.where(row_id == row_of, 1.0, 0.0).astype(BF16)
        incl_t = lax.dot_general(lane_incl.astype(BF16), onehot, tn, preferred_element_type=F32)
        lane_of = jnp.sum(jnp.where(incl_t <= local, 1.0, 0.0), axis=0, keepdims=True)
        idx_ref[0, e:e + 1, :] = (row_of * LANES + lane_of).astype(jnp.int32)
        a_hi = a_e.astype(BF16)
        a_mid = (a_e - a_hi.astype(F32)).astype(BF16)
        a_lo = (a_e - a_hi.astype(F32) - a_mid.astype(F32)).astype(BF16)
        a_t = (lax.dot_general(a_hi, onehot, tn, preferred_element_type=F32)
               + lax.dot_general(a_mid, onehot, tn, preferred_element_type=F32)
               + lax.dot_general(a_lo, onehot, tn, preferred_element_type=F32))
        gate_ref[0, e:e + 1, :] = jnp.sum(jnp.where(lane_id == lane_of, a_t, 0.0), axis=0, keepdims=True)


def expert_select(aff_t, cap):
    n_batch, n_exp, seq = aff_t.shape
    rows = n_exp * seq // LANES
    kern = functools.partial(_select_kernel, cap=cap)
    return pl.pallas_call(
        kern,
        out_shape=(jax.ShapeDtypeStruct((n_batch, n_exp, cap), jnp.int32),
                   jax.ShapeDtypeStruct((n_batch, n_exp, cap), F32)),
        grid=(n_batch,),
        in_specs=[pl.BlockSpec((1, rows, LANES), lambda b: (b, 0, 0)),
                  pl.BlockSpec((1, n_exp, seq), lambda b: (b, 0, 0))],
        out_specs=(pl.BlockSpec((1, n_exp, cap), lambda b: (b, 0, 0)),
                   pl.BlockSpec((1, n_exp, cap), lambda b: (b, 0, 0))),
        compiler_params=_params("parallel"),
        name="expert_select",
    )(aff_t.reshape(n_batch, rows, LANES), aff_t)


DMA_UNROLL = 8


def _cast_kernel(x_ref, o_ref):
    o_ref[...] = x_ref[...].astype(o_ref.dtype)


def cast_layer_bf16(w, layer):
    _, n, r, c = w.shape
    return pl.pallas_call(
        _cast_kernel,
        out_shape=jax.ShapeDtypeStruct((n, r, c), BF16),
        grid=(n,),
        in_specs=[pl.BlockSpec((None, 1, r, c), lambda i: (layer, i, 0, 0))],
        out_specs=pl.BlockSpec((1, r, c), lambda i: (i, 0, 0)),
        compiler_params=_params("parallel"),
        name="cast_bf16",
    )(w)


def _moe_kernel(idx_ref, gate_ref, g_ref, wg_ref, wu_ref, wd_ref, hin_ref, acc_in_ref, acc_ref,
                x_buf, o_buf, sem_x, sem_o, sem_s, *, seq, cap, n_batch):
    del acc_in_ref
    e, b, ti = pl.program_id(0), pl.program_id(1), pl.program_id(2)
    n_exp, nb, nt = pl.num_programs(0), pl.num_programs(1), pl.num_programs(2)
    tm = x_buf.shape[1]
    step = (e * nb + b) * nt + ti
    last = n_exp * nb * nt - 1
    slot = step % 2
    other = 1 - slot
    nxt = jnp.minimum(step + 1, last)
    e_n, b_n, ti_n = nxt // (nb * nt), (nxt // nt) % nb, nxt % nt
    base = (b * n_exp + e) * cap + ti * tm
    base_n = (b_n * n_exp + e_n) * cap + ti_n * tm

    def x_copy(j, src_base, src_batch, dst_slot):
        row = pl.ds(src_batch * seq + idx_ref[src_base + j], 1)
        return pltpu.make_async_copy(hin_ref.at[row], x_buf.at[dst_slot, pl.ds(j, 1)], sem_x.at[dst_slot])

    def acc_rows(j):
        return acc_ref.at[pl.ds(b * seq + idx_ref[base + j], 1)]

    def wait_x(s):
        pltpu.make_async_copy(hin_ref.at[pl.ds(0, tm)], x_buf.at[s], sem_x.at[s]).wait()

    def wait_scatter(s):
        pltpu.make_async_copy(o_buf.at[s], acc_ref.at[pl.ds(0, tm)], sem_s.at[s]).wait()

    @pl.when(step == 0)
    def _():
        def first(jj, c):
            for u in range(DMA_UNROLL):
                x_copy(jj * DMA_UNROLL + u, base, b, slot).start()
            return c
        lax.fori_loop(0, tm // DMA_UNROLL, first, 0)

    if n_batch == 1:
        @pl.when(step > 0)
        def _():
            wait_scatter(other)

    wait_x(slot)
    for j in range(tm):
        x_copy(j, base_n, b_n, other).start(priority=j % 2)
        pltpu.make_async_copy(acc_rows(j), o_buf.at[slot, pl.ds(j, 1)], sem_o.at[slot]).start(priority=(j + 1) % 2)
    x = _rmsnorm_rows(x_buf[slot], g_ref[...]).astype(BF16)
    a = jnp.dot(x, wg_ref[0], preferred_element_type=F32)
    u = jnp.dot(x, wu_ref[0], preferred_element_type=F32)
    hmid = (a * (1.0 / (1.0 + jnp.exp(-a))) * u).astype(BF16)
    y = jnp.dot(hmid, wd_ref[0], preferred_element_type=F32)
    pltpu.make_async_copy(acc_ref.at[pl.ds(0, tm)], o_buf.at[slot], sem_o.at[slot]).wait()
    o_buf[slot] = o_buf[slot] + y * gate_ref[...]

    def start_scatter(jj, c):
        for u in range(DMA_UNROLL):
            j = jj * DMA_UNROLL + u
            pltpu.make_async_copy(o_buf.at[slot, pl.ds(j, 1)], acc_rows(j), sem_s.at[slot]).start(priority=u % 2)
        return c

    lax.fori_loop(0, tm // DMA_UNROLL, start_scatter, 0)

    if n_batch > 1:
        @pl.when(step > 0)
        def _():
            wait_scatter(other)

    @pl.when(step == last)
    def _():
        wait_x(other)
        wait_scatter(slot)


def moe_ffn(h, acc, g, idx, gate, w_gate, w_up, w_down, n_batch):
    t, d = h.shape
    seq = t // n_batch
    n_exp, _, ff = w_gate.shape
    cap = idx.shape[-1]
    tm = _row_tile(cap, 512)
    nt = cap // tm
    kern = functools.partial(_moe_kernel, seq=seq, cap=cap, n_batch=n_batch)
    grid_spec = pltpu.PrefetchScalarGridSpec(
        num_scalar_prefetch=1,
        grid=(n_exp, n_batch, nt),
        in_specs=[pl.BlockSpec((tm, 1), lambda e, b, i, idx: ((b * n_exp + e) * nt + i, 0)),
                  pl.BlockSpec((1, d), lambda e, b, i, idx: (0, 0)),
                  pl.BlockSpec((1, d, ff), lambda e, b, i, idx: (e, 0, 0)),
                  pl.BlockSpec((1, d, ff), lambda e, b, i, idx: (e, 0, 0)),
                  pl.BlockSpec((1, ff, d), lambda e, b, i, idx: (e, 0, 0)),
                  pl.BlockSpec(memory_space=pl.ANY),
                  pl.BlockSpec(memory_space=pl.ANY)],
        out_specs=pl.BlockSpec(memory_space=pl.ANY),
        scratch_shapes=[pltpu.VMEM((2, tm, d), F32), pltpu.VMEM((2, tm, d), F32), pltpu.SemaphoreType.DMA((2,)),
                        pltpu.SemaphoreType.DMA((2,)), pltpu.SemaphoreType.DMA((2,))],
    )
    return pl.pallas_call(
        kern,
        out_shape=jax.ShapeDtypeStruct((t, d), F32),
        grid_spec=grid_spec,
        input_output_aliases={7: 0},
        compiler_params=_params("arbitrary", "arbitrary", "arbitrary"),
        name="moe_ffn",
    )(idx.reshape(-1), gate.reshape(-1, 1), g.reshape(1, d), w_gate, w_up, w_down, h, acc)


def _even_layer(h, n_batch, seq, g_mix, w_in, w_out, s5, d_skip, w_glu, b_glu, pool_w, pool_scale):
    t = h.shape[0]
    z = norm_proj(h, g_mix, w_in.astype(BF16), F32)
    half = z.shape[1] // 2
    n_grp = half // S5_GROUP
    nc = t // S5_CHUNK
    u = z[:, :half].astype(BF16).reshape(nc, S5_CHUNK, n_grp, S5_GROUP)
    u = u.transpose(2, 0, 1, 3).reshape(n_grp, nc, S5_CHUNK * S5_GROUP)
    wt, wst, wout, al = s5_chunk_matrices(*s5)
    ys = s5_scan(u, wt.astype(BF16), wst.astype(BF16), wout.astype(BF16), al, n_batch)
    ys = ys.reshape(n_grp, nc, S5_CHUNK, S5_GROUP).transpose(1, 2, 0, 3).reshape(t, half)
    return even_post(z, ys, d_skip.astype(F32), w_glu.astype(BF16), b_glu.astype(F32), pool_w.astype(BF16),
                     pool_scale.astype(F32), w_out.astype(BF16), h, seq)


def _odd_layer(h, n_batch, layer, g_mix, cos, sin, w_qkv, w_out, q_gain, k_gain, lq1, lk1, lq2, lk2, subln_g):
    q, k, vt = attn_qkv(h, g_mix, w_qkv, cos, sin, q_gain, k_gain)
    lam_init = 0.8 - 0.6 * math.exp(-0.3 * layer)
    o = flash_diff_attention(q, k, vt, lq1, lk1, lq2, lk2, subln_g, n_batch, lam_init)
    return proj_residual(o, w_out.astype(BF16), h)


def _moe_layer(h, acc, n_batch, layer, g_ffn, w_router, w_gate_all, w_up_all, w_down_all):
    seq = h.shape[0] // n_batch
    cap = CAPACITY_FACTOR * seq // N_EXPERTS
    aff_t = router(h, g_ffn, w_router, n_batch)
    idx, gate = expert_select(aff_t, cap)
    return moe_ffn(h, acc, g_ffn, idx, gate, cast_layer_bf16(w_gate_all, layer), cast_layer_bf16(w_up_all, layer),
                   cast_layer_bf16(w_down_all, layer), n_batch)


def kernel(x, positions, norm_mix_g, norm_ffn_g, hyb_w_in, hyb_w_out, s5_lam_re, s5_lam_im, s5_log_dt, s5_b_re, s5_b_im, s5_c_re, s5_c_im, s5_d, s5_w_glu, s5_b_glu, pool_w, pool_scale, attn_w_qkv, attn_w_out, attn_q_norm_g, attn_k_norm_g, attn_lam_q1, attn_lam_k1, attn_lam_q2, attn_lam_k2, attn_subln_g, moe_w_router, moe_w_gate, moe_w_up, moe_w_down):
    n_batch, seq, d = x.shape
    depth = norm_mix_g.shape[0]
    h = x.reshape(n_batch * seq, d)
    cos, sin = rope_tables(positions)
    for layer in range(depth):
        if layer % 2 == 0:
            e = layer // 2
            s5 = (s5_lam_re[e], s5_lam_im[e], s5_log_dt[e], s5_b_re[e], s5_b_im[e], s5_c_re[e], s5_c_im[e])
            h, acc = _even_layer(h, n_batch, seq, norm_mix_g[layer], hyb_w_in[e], hyb_w_out[e], s5, s5_d[e],
                                 s5_w_glu[e], s5_b_glu[e], pool_w[e], pool_scale[e])
        else:
            o = layer // 2
            h, acc = _odd_layer(h, n_batch, layer, norm_mix_g[layer], cos, sin, attn_w_qkv[o], attn_w_out[o],
                                attn_q_norm_g[o], attn_k_norm_g[o], attn_lam_q1[o], attn_lam_k1[o],
                                attn_lam_q2[o], attn_lam_k2[o], attn_subln_g[o])
        h = _moe_layer(h, acc, n_batch, layer, norm_ffn_g[layer], moe_w_router[layer], moe_w_gate,
                       moe_w_up, moe_w_down)
    return h.reshape(n_batch, seq, d)
```

```python
import functools
import math

import jax
import jax.numpy as jnp
from jax import lax
from jax.experimental import pallas as pl
from jax.experimental.pallas import tpu as pltpu

F32 = jnp.float32
BF16 = jnp.bfloat16

EPS = 1e-6
LANES = 128
VMEM_LIMIT = 56 * 1024 * 1024

S5_GROUP = 16
S5_CHUNK = 16
S5_GROUPS_PER_STEP = 4
POOL_WINDOWS = (2, 4, 8, 16)
POOL_HALO = 8
ATT_HEAD_DIM = 64
ROPE_THETA = 10000.0
N_EXPERTS = 16
CAPACITY_FACTOR = 2


def _row_tile(n, want):
    t = min(n, want)
    assert n % t == 0, (n, t)
    return t


def _params(*sem):
    return pltpu.CompilerParams(dimension_semantics=sem, vmem_limit_bytes=VMEM_LIMIT)


def _rmsnorm_rows(x, g):
    return x * lax.rsqrt(jnp.mean(x * x, axis=-1, keepdims=True) + EPS) * g


def _split_bf16(x):
    hi = x.astype(BF16)
    lo = (x - hi.astype(F32)).astype(BF16)
    return hi, lo


def _norm_proj_kernel(h_ref, g_ref, w_ref, o_ref):
    y = _rmsnorm_rows(h_ref[...], g_ref[...])
    o_ref[...] = jnp.dot(y.astype(BF16), w_ref[...], preferred_element_type=F32).astype(o_ref.dtype)


def norm_proj(h, g, w, out_dtype):
    t, d = h.shape
    n = w.shape[1]
    tm = _row_tile(t, 512)
    return pl.pallas_call(
        _norm_proj_kernel,
        out_shape=jax.ShapeDtypeStruct((t, n), out_dtype),
        grid=(t // tm,),
        in_specs=[pl.BlockSpec((tm, d), lambda i: (i, 0)),
                  pl.BlockSpec((1, d), lambda i: (0, 0)),
                  pl.BlockSpec((d, n), lambda i: (0, 0))],
        out_specs=pl.BlockSpec((tm, n), lambda i: (i, 0)),
        compiler_params=_params("parallel"),
        name="norm_proj",
    )(h, g.reshape(1, d), w)


def _s5_kernel(u_ref, wt_ref, wst_ref, wout_ref, al_ref, y_ref, s_ref, hf_ref, hb_ref, *, n_batch):
    n_grp, nc_all, _ = u_ref.shape
    nc = nc_all // n_batch
    for gi in range(n_grp):
        s_ref[gi] = jnp.dot(u_ref[gi], wst_ref[gi], preferred_element_type=F32)
    is_fwd = lax.broadcasted_iota(jnp.int32, (1, 2 * LANES), 1) % LANES < LANES // 2

    def step(i, carry):
        new = []
        for gi in range(n_grp):
            a_re = al_ref[gi, 0:1, :]
            a_im = al_ref[gi, 1:2, :]
            for b in range(n_batch):
                h = carry[gi * n_batch + b]
                row_f = b * nc + i
                row_b = b * nc + nc - 1 - i
                hf_ref[gi, pl.ds(row_f, 1), :] = h
                hb_ref[gi, pl.ds(row_b, 1), :] = h
                s = jnp.where(is_fwd, s_ref[gi, pl.ds(row_f, 1), :], s_ref[gi, pl.ds(row_b, 1), :])
                h_re = h[:, :LANES]
                h_im = h[:, LANES:]
                n_re = a_re * h_re - a_im * h_im + s[:, :LANES]
                n_im = a_re * h_im + a_im * h_re + s[:, LANES:]
                new.append(jnp.concatenate([n_re, n_im], axis=1))
        return tuple(new)

    zero = jnp.zeros((1, 2 * LANES), F32)
    lax.fori_loop(0, nc, step, tuple(zero for _ in range(n_grp * n_batch)))
    for gi in range(n_grp):
        h_in = jnp.where(is_fwd, hf_ref[gi], hb_ref[gi])
        y = jnp.dot(u_ref[gi], wt_ref[gi], preferred_element_type=F32)
        y = y + jnp.dot(h_in.astype(BF16), wout_ref[gi], preferred_element_type=F32)
        y_ref[gi] = y.astype(y_ref.dtype)


def s5_scan(u, wt, wst, wout, al, n_batch):
    g, nc_all, w = u.shape
    kern = functools.partial(_s5_kernel, n_batch=n_batch)
    gb = S5_GROUPS_PER_STEP if g % S5_GROUPS_PER_STEP == 0 else 1
    mat = pl.BlockSpec((gb, w, w), lambda i: (i, 0, 0))
    return pl.pallas_call(
        kern,
        out_shape=jax.ShapeDtypeStruct((g, nc_all, w), BF16),
        grid=(g // gb,),
        in_specs=[pl.BlockSpec((gb, nc_all, w), lambda i: (i, 0, 0)), mat, mat, mat,
                  pl.BlockSpec((gb, 2, LANES), lambda i: (i, 0, 0))],
        out_specs=pl.BlockSpec((gb, nc_all, w), lambda i: (i, 0, 0)),
        scratch_shapes=[pltpu.VMEM((gb, nc_all, w), F32), pltpu.VMEM((gb, nc_all, w), F32),
                        pltpu.VMEM((gb, nc_all, w), F32)],
        compiler_params=_params("parallel"),
        name="s5_scan",
    )(u, wt, wst, wout, al)


def s5_chunk_matrices(lam_re, lam_im, log_dt, b_re, b_im, c_re, c_im):
    L = S5_CHUNK
    lam = lax.complex(lam_re.astype(F32), lam_im.astype(F32))
    dt = jnp.exp(log_dt.astype(F32))[..., None]
    lam_dt = lam * dt
    a_bar = jnp.exp(lam_dt)
    b = lax.complex(b_re.astype(F32), b_im.astype(F32))
    b_bar = ((a_bar - 1.0) / lam)[..., None] * b
    c = lax.complex(c_re.astype(F32), c_im.astype(F32))
    k = jnp.arange(L + 1, dtype=F32)
    pw = jnp.exp(lam_dt[:, :, None, :] * k[None, None, :, None])
    n_grp, n_state = lam.shape[1], lam.shape[2]
    kern = jnp.einsum('dgop,dgkp,dgpi->dgkoi', c, pw[:, :, :L], b_bar).real
    t_idx = jnp.arange(L)[None, :]
    s_idx = jnp.arange(L)[:, None]
    lag_f = t_idx - s_idx
    kf = jnp.where((lag_f >= 0)[None, :, :, None, None], kern[0][:, jnp.clip(lag_f, 0, L - 1)], 0.0)
    kb = jnp.where((lag_f <= 0)[None, :, :, None, None], kern[1][:, jnp.clip(-lag_f, 0, L - 1)], 0.0)
    wt = (kf + kb).transpose(0, 1, 4, 2, 3).reshape(n_grp, L * S5_GROUP, L * S5_GROUP)
    cf = pw[0][:, L - 1 - jnp.arange(L)][..., None] * b_bar[0][:, None]
    cb = pw[1][:, jnp.arange(L)][..., None] * b_bar[1][:, None]
    def st(x):
        return x.transpose(0, 1, 3, 2).reshape(n_grp, L * S5_GROUP, n_state)
    wst = jnp.concatenate([st(cf.real), st(cb.real), st(cf.imag), st(cb.imag)], axis=-1)
    of = c[0][:, None] * pw[0][:, 1 + jnp.arange(L)][:, :, None, :]
    ob = c[1][:, None] * pw[1][:, L - jnp.arange(L)][:, :, None, :]
    def ot(x):
        return x.transpose(0, 3, 1, 2).reshape(n_grp, n_state, L * S5_GROUP)
    wout = jnp.concatenate([ot(of.real), ot(ob.real), ot(-of.imag), ot(-ob.imag)], axis=1)
    a_l = pw[:, :, L]
    al = jnp.stack([jnp.concatenate([a_l[0].real, a_l[1].real], axis=-1),
                    jnp.concatenate([a_l[0].imag, a_l[1].imag], axis=-1)], axis=1)
    return wt, wst, wout, al


def _gelu_tanh(x):
    return 0.5 * x * (1.0 + jnp.tanh(math.sqrt(2.0 / math.pi) * (x + 0.044715 * (x * x * x))))


def _even_post_kernel(z_ref, zp_ref, zn_ref, ys_ref, d_ref, wglu_ref, bglu_ref, pw_ref, ps_ref, wout_ref,
                      h_ref, o_ref, o2_ref, ext_ref, cat_ref, *, seq):
    tm = z_ref.shape[0]
    half = z_ref.shape[1] // 2
    t0 = (pl.program_id(0) * tm) % seq
    y = d_ref[...] * z_ref[:, :half] + ys_ref[...].astype(F32)
    y = _gelu_tanh(y)
    gate = jnp.dot(y.astype(BF16), wglu_ref[...], preferred_element_type=F32) + bglu_ref[...]
    cat_ref[:, :half] = (y * (1.0 / (1.0 + jnp.exp(-gate)))).astype(BF16)
    v = z_ref[:, half:]
    ext_ref[0:POOL_HALO, :] = jnp.where(t0 > 0, zp_ref[...], 0.0)
    ext_ref[POOL_HALO:POOL_HALO + tm, :] = v
    ext_ref[POOL_HALO + tm:, :] = jnp.where(t0 + tm < seq, zn_ref[...], 0.0)
    pos = t0 + lax.broadcasted_iota(jnp.int32, (tm, 1), 0)
    grp = half // len(POOL_WINDOWS)
    for gi, win in enumerate(POOL_WINDOWS):
        cols = slice(gi * grp, (gi + 1) * grp)
        acc = ext_ref[POOL_HALO - win // 2:POOL_HALO - win // 2 + tm, cols]
        for j in range(1, win):
            off = POOL_HALO - win // 2 + j
            acc = acc + ext_ref[off:off + tm, cols]
        cnt = jnp.minimum(pos + win // 2, seq) - jnp.maximum(pos - win // 2, 0)
        pooled = acc / cnt.astype(F32) - v[:, cols]
        yb = jnp.dot(pooled.astype(BF16), pw_ref[gi], preferred_element_type=F32) * ps_ref[:, cols]
        cat_ref[:, half + gi * grp:half + (gi + 1) * grp] = yb.astype(BF16)
    out = h_ref[...] + jnp.dot(cat_ref[...], wout_ref[...], preferred_element_type=F32)
    o_ref[...] = out
    o2_ref[...] = out


def even_post(z, ys, d_skip, w_glu, b_glu, pool_w, pool_scale, w_out, h, seq):
    t, width = z.shape
    half = width // 2
    d = h.shape[1]
    tm = _row_tile(seq, 512)
    hb = tm // POOL_HALO
    nblk8 = t // POOL_HALO
    kern = functools.partial(_even_post_kernel, seq=seq)
    full = lambda shape: pl.BlockSpec(shape, lambda i: (0,) * len(shape))
    return pl.pallas_call(
        kern,
        out_shape=(jax.ShapeDtypeStruct((t, d), F32), jax.ShapeDtypeStruct((t, d), F32)),
        grid=(t // tm,),
        in_specs=[pl.BlockSpec((tm, width), lambda i: (i, 0)),
                  pl.BlockSpec((POOL_HALO, half), lambda i: (jnp.maximum(i * hb - 1, 0), 1)),
                  pl.BlockSpec((POOL_HALO, half), lambda i: (jnp.minimum((i + 1) * hb, nblk8 - 1), 1)),
                  pl.BlockSpec((tm, half), lambda i: (i, 0)),
                  full((1, half)), full((half, half)), full((1, half)),
                  full(pool_w.shape), full((1, half)), full((width, d)),
                  pl.BlockSpec((tm, d), lambda i: (i, 0))],
        out_specs=(pl.BlockSpec((tm, d), lambda i: (i, 0)), pl.BlockSpec((tm, d), lambda i: (i, 0))),
        scratch_shapes=[pltpu.VMEM((tm + 2 * POOL_HALO, half), F32), pltpu.VMEM((tm, width), BF16)],
        compiler_params=_params("parallel"),
        name="even_post",
    )(z, z, z, ys, d_skip.reshape(1, half), w_glu, b_glu.reshape(1, half), pool_w,
      pool_scale.reshape(1, half), w_out, h)


def _rope_kernel(pos_ref, inv_ref, sign_ref, cos_ref, sin_ref):
    ang = pos_ref[...].astype(F32) * inv_ref[...]
    cos_ref[...] = jnp.cos(ang)
    sin_ref[...] = jnp.sin(ang) * sign_ref[...]


def rope_tables(positions):
    t = positions.size
    half = ATT_HEAD_DIM // 2
    inv = ROPE_THETA ** (-jnp.arange(0, ATT_HEAD_DIM, 2, dtype=F32) / ATT_HEAD_DIM)
    reps = LANES // half
    inv_pat = jnp.tile(inv, reps).reshape(1, LANES)
    sign = jnp.tile(jnp.concatenate([-jnp.ones((half,), F32), jnp.ones((half,), F32)]), reps // 2).reshape(1, LANES)
    pos = jnp.broadcast_to(positions.reshape(t, 1), (t, LANES))
    tm = _row_tile(t, 1024)
    return pl.pallas_call(
        _rope_kernel,
        out_shape=(jax.ShapeDtypeStruct((t, LANES), F32), jax.ShapeDtypeStruct((t, LANES), F32)),
        grid=(t // tm,),
        in_specs=[pl.BlockSpec((tm, LANES), lambda i: (i, 0)),
                  pl.BlockSpec((1, LANES), lambda i: (0, 0)),
                  pl.BlockSpec((1, LANES), lambda i: (0, 0))],
        out_specs=(pl.BlockSpec((tm, LANES), lambda i: (i, 0)), pl.BlockSpec((tm, LANES), lambda i: (i, 0))),
        compiler_params=_params("parallel"),
        name="rope_tables",
    )(pos, inv_pat, sign)


ATT_KV_CHUNK = 512


def _qkv_kernel(h_ref, g_ref, wqk_ref, wvt_ref, cos_ref, sin_ref, qg_ref, kg_ref, seg_ref, q_ref, k_ref, vt_ref):
    d = h_ref.shape[1]
    y = _rmsnorm_rows(h_ref[...], g_ref[...]).astype(BF16)
    cos = cos_ref[...]
    sin = sin_ref[...]
    seg = seg_ref[...]
    first_half = lax.broadcasted_iota(jnp.int32, (1, LANES), 1) % ATT_HEAD_DIM < ATT_HEAD_DIM // 2
    q_scale = ATT_HEAD_DIM ** -0.5 * math.log2(math.e)
    for which, (o_ref, gain_ref, scale) in enumerate(((q_ref, qg_ref, q_scale), (k_ref, kg_ref, 1.0))):
        x_all = jnp.dot(y, wqk_ref[:, which * d:(which + 1) * d], preferred_element_type=F32)
        for j in range(d // LANES):
            x = x_all[:, j * LANES:(j + 1) * LANES]
            hi, lo = _split_bf16(x * x)
            ss = jnp.dot(hi, seg, preferred_element_type=F32) + jnp.dot(lo, seg, preferred_element_type=F32)
            xn = x * lax.rsqrt(ss * (1.0 / ATT_HEAD_DIM) + EPS) * gain_ref[...]
            swapped = jnp.where(first_half, pltpu.roll(xn, LANES - ATT_HEAD_DIM // 2, 1),
                                pltpu.roll(xn, ATT_HEAD_DIM // 2, 1))
            o_ref[:, j * LANES:(j + 1) * LANES] = ((xn * cos + swapped * sin) * scale).astype(BF16)
    vt = lax.dot_general(wvt_ref[...], y, (((1,), (1,)), ((), ())), preferred_element_type=F32)
    vt_ref[0] = vt.reshape(vt_ref.shape[1:]).astype(BF16)


def attn_qkv(h, g, w_qkv, cos, sin, q_gain, k_gain):
    t, d = h.shape
    tm = _row_tile(t, ATT_KV_CHUNK)
    heads = d // LANES
    reps = LANES // ATT_HEAD_DIM
    lane = jnp.arange(LANES)
    seg = (lane[:, None] // ATT_HEAD_DIM == lane[None, :] // ATT_HEAD_DIM).astype(BF16)
    w_qk = w_qkv[:, :2 * d].astype(BF16)
    w_vt = w_qkv[:, 2 * d:].T.astype(BF16)
    row = lambda n: pl.BlockSpec((tm, n), lambda i: (i, 0))
    one = lambda n: pl.BlockSpec((1, n), lambda i: (0, 0))
    out = jax.ShapeDtypeStruct((t, d), BF16)
    return pl.pallas_call(
        _qkv_kernel,
        out_shape=(out, out, jax.ShapeDtypeStruct((t // tm, heads, LANES, tm), BF16)),
        grid=(t // tm,),
        in_specs=[row(d), one(d), pl.BlockSpec(w_qk.shape, lambda i: (0, 0)), pl.BlockSpec(w_vt.shape, lambda i: (0, 0)),
                  row(LANES), row(LANES), one(LANES), one(LANES), pl.BlockSpec((LANES, LANES), lambda i: (0, 0))],
        out_specs=(row(d), row(d), pl.BlockSpec((1, heads, LANES, tm), lambda i: (i, 0, 0, 0))),
        compiler_params=_params("parallel"),
        name="attn_qkv",
    )(h, g.reshape(1, d), w_qk, w_vt, cos, sin, jnp.tile(q_gain.astype(F32), reps).reshape(1, LANES),
      jnp.tile(k_gain.astype(F32), reps).reshape(1, LANES), seg)


def _flash_kernel(q_ref, k_ref, vt_ref, lq1_ref, lk1_ref, lq2_ref, lk2_ref, sg_ref, o_ref, acc_ref, sta_ref, stb_ref,
                  *, lam_init):
    tq = q_ref.shape[0]
    nk, _, _, tk = vt_ref.shape
    q = q_ref[...]
    first = lax.broadcasted_iota(jnp.int32, (1, LANES), 1) < ATT_HEAD_DIM
    zero = jnp.zeros_like(q)
    qm = jnp.concatenate([jnp.where(first, q, zero), jnp.where(first, zero, q)], axis=0)
    acc_ref[...] = jnp.zeros(acc_ref.shape, F32)
    nt = (((1,), (1,)), ((), ()))

    def scores(i, st_ref):
        kc = k_ref[pl.ds(pl.multiple_of(i * tk, tk), tk), :]
        st = lax.dot_general(kc, qm, nt, preferred_element_type=F32)
        st_ref[...] = st
        return jnp.max(st, axis=0, keepdims=True)

    def absorb(i, st_ref, chunk_max, m_prev, l8):
        m_new = jnp.maximum(m_prev, chunk_max)
        alpha = jnp.exp2(m_prev - m_new)
        p = jnp.exp2(st_ref[...] - m_new)
        l8 = alpha * l8 + jnp.sum(p.reshape(tk // 8, 8, 2 * tq), axis=0)
        acc_ref[...] = alpha * acc_ref[...] + jnp.dot(vt_ref[i, 0], p.astype(BF16), preferred_element_type=F32)
        return m_new, l8

    per_trip = 16 if nk % 16 == 0 else (8 if nk % 8 == 0 else 2)
    assert nk % per_trip == 0

    def body(j, carry):
        m, l8, max_cur = carry
        bufs = (sta_ref, stb_ref)
        for u in range(per_trip):
            i = per_trip * j + u
            nxt = i + 1 if u + 1 < per_trip else jnp.minimum(i + 1, nk - 1)
            max_nxt = scores(nxt, bufs[(u + 1) % 2])
            m, l8 = absorb(i, bufs[u % 2], max_cur, m, l8)
            max_cur = max_nxt
        return m, l8, max_cur

    m0 = jnp.full((1, 2 * tq), -jnp.inf, F32)
    _, l8, _ = lax.fori_loop(0, nk // per_trip, body, (m0, jnp.zeros((8, 2 * tq), F32), scores(0, sta_ref)))
    lam = (jnp.exp(jnp.sum(lq1_ref[...] * lk1_ref[...], axis=-1, keepdims=True))
           - jnp.exp(jnp.sum(lq2_ref[...] * lk2_ref[...], axis=-1, keepdims=True)) + lam_init)
    o = acc_ref[...] / jnp.sum(l8, axis=0, keepdims=True)
    o = o[:, :tq] - lam * o[:, tq:]
    o = o * lax.rsqrt(jnp.mean(o * o, axis=0, keepdims=True) + EPS) * sg_ref[...]
    o_ref[...] = (o * (1.0 - lam_init)).T.astype(o_ref.dtype)


def flash_diff_attention(q, k, vt, lam_q1, lam_k1, lam_q2, lam_k2, subln_g, n_batch, lam_init):
    t, width = q.shape
    seq = t // n_batch
    heads = width // LANES
    tk = vt.shape[-1]
    tq = _row_tile(seq, 256)
    nq, nk = seq // tq, seq // tk
    kern = functools.partial(_flash_kernel, lam_init=lam_init)
    vec = lambda n: pl.BlockSpec((1, n), lambda b, h, qi: (0, 0))
    lam_vec = lambda x: x.astype(F32).reshape(1, ATT_HEAD_DIM)
    return pl.pallas_call(
        kern,
        out_shape=jax.ShapeDtypeStruct((t, width), BF16),
        grid=(n_batch, heads, nq),
        in_specs=[pl.BlockSpec((tq, LANES), lambda b, h, qi: (b * nq + qi, h)),
                  pl.BlockSpec((seq, LANES), lambda b, h, qi: (b, h)),
                  pl.BlockSpec((nk, 1, LANES, tk), lambda b, h, qi: (b, h, 0, 0)),
                  vec(ATT_HEAD_DIM), vec(ATT_HEAD_DIM), vec(ATT_HEAD_DIM), vec(ATT_HEAD_DIM),
                  pl.BlockSpec((LANES, 1), lambda b, h, qi: (0, 0))],
        out_specs=pl.BlockSpec((tq, LANES), lambda b, h, qi: (b * nq + qi, h)),
        scratch_shapes=[pltpu.VMEM((LANES, 2 * tq), F32), pltpu.VMEM((tk, 2 * tq), F32),
                        pltpu.VMEM((tk, 2 * tq), F32)],
        compiler_params=_params("parallel", "parallel", "arbitrary"),
        name="flash_diff_attention",
    )(q, k, vt, lam_vec(lam_q1), lam_vec(lam_k1), lam_vec(lam_q2), lam_vec(lam_k2),
      subln_g.astype(F32).reshape(LANES, 1))


def _proj_residual_kernel(x_ref, w_ref, h_ref, o_ref, o2_ref):
    out = h_ref[...] + jnp.dot(x_ref[...], w_ref[...], preferred_element_type=F32)
    o_ref[...] = out
    o2_ref[...] = out


def proj_residual(x, w, h):
    t, d = h.shape
    kdim = x.shape[1]
    tm = _row_tile(t, 512)
    row = lambda n: pl.BlockSpec((tm, n), lambda i: (i, 0))
    out = jax.ShapeDtypeStruct((t, d), F32)
    return pl.pallas_call(
        _proj_residual_kernel,
        out_shape=(out, out),
        grid=(t // tm,),
        in_specs=[row(kdim), pl.BlockSpec((kdim, d), lambda i: (0, 0)), row(d)],
        out_specs=(row(d), row(d)),
        compiler_params=_params("parallel"),
        name="proj_residual",
    )(x, w, h)


def _router_kernel(h_ref, g_ref, wt_hi_ref, wt_lo_ref, aff_ref):
    y = _rmsnorm_rows(h_ref[...], g_ref[...])
    y_hi, y_lo = _split_bf16(y)
    nt = (((1,), (1,)), ((), ()))
    logits = (lax.dot_general(wt_hi_ref[...], y_hi, nt, preferred_element_type=F32)
              + lax.dot_general(wt_hi_ref[...], y_lo, nt, preferred_element_type=F32)
              + lax.dot_general(wt_lo_ref[...], y_hi, nt, preferred_element_type=F32))
    e = jnp.exp(logits - jnp.max(logits, axis=0, keepdims=True))
    aff_ref[0] = e / jnp.sum(e, axis=0, keepdims=True)


def router(h, g, w_router, n_batch):
    t, d = h.shape
    seq = t // n_batch
    n_exp = w_router.shape[1]
    tm = _row_tile(seq, 512)
    ns = seq // tm
    wt = w_router.astype(F32).T
    wt_hi, wt_lo = _split_bf16(wt)
    return pl.pallas_call(
        _router_kernel,
        out_shape=jax.ShapeDtypeStruct((n_batch, n_exp, seq), F32),
        grid=(n_batch, ns),
        in_specs=[pl.BlockSpec((tm, d), lambda b, i: (b * ns + i, 0)),
                  pl.BlockSpec((1, d), lambda b, i: (0, 0)),
                  pl.BlockSpec((n_exp, d), lambda b, i: (0, 0)),
                  pl.BlockSpec((n_exp, d), lambda b, i: (0, 0))],
        out_specs=pl.BlockSpec((1, n_exp, tm), lambda b, i: (b, 0, i)),
        compiler_params=_params("parallel", "parallel"),
        name="router",
    )(h, g.reshape(1, d), wt_hi, wt_lo)


def _select_kernel(aff_ref, affw_ref, idx_ref, gate_ref, *, cap):
    rows_all = aff_ref.shape[1]
    n_exp = idx_ref.shape[1]
    r = rows_all // n_exp
    aff = aff_ref[0]
    bits_wide = pltpu.bitcast(affw_ref[0], jnp.int32)

    def count(mask):
        return jnp.sum(jnp.where(mask, 1.0, 0.0), axis=1, keepdims=True)

    def bit_step(i, thr):
        cand = thr | jnp.left_shift(jnp.int32(1), 30 - i)
        return jnp.where(count(bits_wide >= cand) >= cap, cand, thr)

    thr = lax.fori_loop(0, 31, bit_step, jnp.zeros((n_exp, 1), jnp.int32))
    n_tie_take = cap - count(bits_wide > thr)

    tri_incl = (lax.broadcasted_iota(jnp.int32, (LANES, LANES), 0)
                <= lax.broadcasted_iota(jnp.int32, (LANES, LANES), 1)).astype(BF16)
    ones_mat = jnp.ones((LANES, LANES), BF16)
    low_strict = (lax.broadcasted_iota(jnp.int32, (r, r), 1)
                  < lax.broadcasted_iota(jnp.int32, (r, r), 0)).astype(BF16)

    def prefix(mask2d):
        m = jnp.where(mask2d, 1.0, 0.0).astype(BF16)
        lane_incl = jnp.dot(m, tri_incl, preferred_element_type=F32)
        row_tot = jnp.dot(m, ones_mat, preferred_element_type=F32)
        row_off = jnp.dot(low_strict, row_tot.astype(BF16), preferred_element_type=F32)
        return lane_incl, row_tot, row_off

    slot = lax.broadcasted_iota(jnp.int32, (1, cap), 1).astype(F32)
    row_id = lax.broadcasted_iota(jnp.int32, (r, cap), 0).astype(F32)
    lane_id = lax.broadcasted_iota(jnp.int32, (LANES, cap), 0).astype(F32)
    tn = (((0,), (0,)), ((), ()))
    for e in range(n_exp):
        a_e = aff[e * r:(e + 1) * r, :]
        bits = pltpu.bitcast(a_e, jnp.int32)
        thr_e = thr[e:e + 1, :]
        tie_e = bits == thr_e
        t_incl, _, t_off = prefix(tie_e)
        tie_rank = t_off + t_incl - 1.0
        take = jnp.where(tie_rank < n_tie_take[e:e + 1, :], 1.0, 0.0) * jnp.where(tie_e, 1.0, 0.0)
        sel = jnp.where(bits > thr_e, 1.0, take) > 0.5
        lane_incl, row_tot, row_off = prefix(sel)
        off_b = jnp.tile(row_off, (1, cap // LANES))
        cum_b = jnp.tile(row_off + row_tot, (1, cap // LANES))
        row_of = jnp.sum(jnp.where(cum_b <= slot, 1.0, 0.0), axis=0, keepdims=True)
        off_of = jnp.max(jnp.where(off_b <= slot, off_b, 0.0), axis=0, keepdims=True)
        local = slot - off_of
        onehot = jnp.where(row_id == row_of, 1.0, 0.0).astype(BF16)
        incl_t = lax.dot_general(lane_incl.astype(BF16), onehot, tn, preferred_element_type=F32)
        lane_of = jnp.sum(jnp.where(incl_t <= local, 1.0, 0.0), axis=0, keepdims=True)
        idx_ref[0, e:e + 1, :] = (row_of * LANES + lane_of).astype(jnp.int32)
        a_hi = a_e.astype(BF16)
        a_mid = (a_e - a_hi.astype(F32)).astype(BF16)
        a_lo = (a_e - a_hi.astype(F32) - a_mid.astype(F32)).astype(BF16)
        a_t = (lax.dot_general(a_hi, onehot, tn, preferred_element_type=F32)
               + lax.dot_general(a_mid, onehot, tn, preferred_element_type=F32)
               + lax.dot_general(a_lo, onehot, tn, preferred_element_type=F32))
        gate_ref[0, e:e + 1, :] = jnp.sum(jnp.where(lane_id == lane_of, a_t, 0.0), axis=0, keepdims=True)


def expert_select(aff_t, cap):
    n_batch, n_exp, seq = aff_t.shape
    rows = n_exp * seq // LANES
    kern = functools.partial(_select_kernel, cap=cap)
    return pl.pallas_call(
        kern,
        out_shape=(jax.ShapeDtypeStruct((n_batch, n_exp, cap), jnp.int32),
                   jax.ShapeDtypeStruct((n_batch, n_exp, cap), F32)),
        grid=(n_batch,),
        in_specs=[pl.BlockSpec((1, rows, LANES), lambda b: (b, 0, 0)),
                  pl.BlockSpec((1, n_exp, seq), lambda b: (b, 0, 0))],
        out_specs=(pl.BlockSpec((1, n_exp, cap), lambda b: (b, 0, 0)),
                   pl.BlockSpec((1, n_exp, cap), lambda b: (b, 0, 0))),
        compiler_params=_params("parallel"),
        name="expert_select",
    )(aff_t.reshape(n_batch, rows, LANES), aff_t)


DMA_UNROLL = 8


def _cast_kernel(x_ref, o_ref):
    o_ref[...] = x_ref[...].astype(o_ref.dtype)


def cast_layer_bf16(w, layer):
    _, n, r, c = w.shape
    return pl.pallas_call(
        _cast_kernel,
        out_shape=jax.ShapeDtypeStruct((n, r, c), BF16),
        grid=(n,),
        in_specs=[pl.BlockSpec((None, 1, r, c), lambda i: (layer, i, 0, 0))],
        out_specs=pl.BlockSpec((1, r, c), lambda i: (i, 0, 0)),
        compiler_params=_params("parallel"),
        name="cast_bf16",
    )(w)


def _moe_kernel(idx_ref, gate_ref, g_ref, wg_ref, wu_ref, wd_ref, hin_ref, acc_in_ref, acc_ref,
                x_buf, o_buf, sem_x, sem_o, sem_s, *, seq, cap, n_batch):
    del acc_in_ref
    e, b, ti = pl.program_id(0), pl.program_id(1), pl.program_id(2)
    n_exp, nb, nt = pl.num_programs(0), pl.num_programs(1), pl.num_programs(2)
    tm = x_buf.shape[1]
    step = (e * nb + b) * nt + ti
    last = n_exp * nb * nt - 1
    slot = step % 2
    other = 1 - slot
    nxt = jnp.minimum(step + 1, last)
    e_n, b_n, ti_n = nxt // (nb * nt), (nxt // nt) % nb, nxt % nt
    base = (b * n_exp + e) * cap + ti * tm
    base_n = (b_n * n_exp + e_n) * cap + ti_n * tm

    def x_copy(j, src_base, src_batch, dst_slot):
        row = pl.ds(src_batch * seq + idx_ref[src_base + j], 1)
        return pltpu.make_async_copy(hin_ref.at[row], x_buf.at[dst_slot, pl.ds(j, 1)], sem_x.at[dst_slot])

    def acc_rows(j):
        return acc_ref.at[pl.ds(b * seq + idx_ref[base + j], 1)]

    def wait_x(s):
        pltpu.make_async_copy(hin_ref.at[pl.ds(0, tm)], x_buf.at[s], sem_x.at[s]).wait()

    def wait_scatter(s):
        pltpu.make_async_copy(o_buf.at[s], acc_ref.at[pl.ds(0, tm)], sem_s.at[s]).wait()

    @pl.when(step == 0)
    def _():
        def first(jj, c):
            for u in range(DMA_UNROLL):
                x_copy(jj * DMA_UNROLL + u, base, b, slot).start()
            return c
        lax.fori_loop(0, tm // DMA_UNROLL, first, 0)

    if n_batch == 1:
        @pl.when(step > 0)
        def _():
            wait_scatter(other)

    wait_x(slot)
    for j in range(tm):
        x_copy(j, base_n, b_n, other).start(priority=j % 2)
        pltpu.make_async_copy(acc_rows(j), o_buf.at[slot, pl.ds(j, 1)], sem_o.at[slot]).start(priority=(j + 1) % 2)
    x = _rmsnorm_rows(x_buf[slot], g_ref[...]).astype(BF16)
    a = jnp.dot(x, wg_ref[0], preferred_element_type=F32)
    u = jnp.dot(x, wu_ref[0], preferred_element_type=F32)
    hmid = (a * (1.0 / (1.0 + jnp.exp(-a))) * u).astype(BF16)
    y = jnp.dot(hmid, wd_ref[0], preferred_element_type=F32)
    pltpu.make_async_copy(acc_ref.at[pl.ds(0, tm)], o_buf.at[slot], sem_o.at[slot]).wait()
    o_buf[slot] = o_buf[slot] + y * gate_ref[...]

    def start_scatter(jj, c):
        for u in range(DMA_UNROLL):
            j = jj * DMA_UNROLL + u
            pltpu.make_async_copy(o_buf.at[slot, pl.ds(j, 1)], acc_rows(j), sem_s.at[slot]).start(priority=u % 2)
        return c

    lax.fori_loop(0, tm // DMA_UNROLL, start_scatter, 0)

    if n_batch > 1:
        @pl.when(step > 0)
        def _():
            wait_scatter(other)

    @pl.when(step == last)
    def _():
        wait_x(other)
        wait_scatter(slot)


def moe_ffn(h, acc, g, idx, gate, w_gate, w_up, w_down, n_batch):
    t, d = h.shape
    seq = t // n_batch
    n_exp, _, ff = w_gate.shape
    cap = idx.shape[-1]
    tm = _row_tile(cap, 512)
    nt = cap // tm
    kern = functools.partial(_moe_kernel, seq=seq, cap=cap, n_batch=n_batch)
    grid_spec = pltpu.PrefetchScalarGridSpec(
        num_scalar_prefetch=1,
        grid=(n_exp, n_batch, nt),
        in_specs=[pl.BlockSpec((tm, 1), lambda e, b, i, idx: ((b * n_exp + e) * nt + i, 0)),
                  pl.BlockSpec((1, d), lambda e, b, i, idx: (0, 0)),
                  pl.BlockSpec((1, d, ff), lambda e, b, i, idx: (e, 0, 0)),
                  pl.BlockSpec((1, d, ff), lambda e, b, i, idx: (e, 0, 0)),
                  pl.BlockSpec((1, ff, d), lambda e, b, i, idx: (e, 0, 0)),
                  pl.BlockSpec(memory_space=pl.ANY),
                  pl.BlockSpec(memory_space=pl.ANY)],
        out_specs=pl.BlockSpec(memory_space=pl.ANY),
        scratch_shapes=[pltpu.VMEM((2, tm, d), F32), pltpu.VMEM((2, tm, d), F32), pltpu.SemaphoreType.DMA((2,)),
                        pltpu.SemaphoreType.DMA((2,)), pltpu.SemaphoreType.DMA((2,))],
    )
    return pl.pallas_call(
        kern,
        out_shape=jax.ShapeDtypeStruct((t, d), F32),
        grid_spec=grid_spec,
        input_output_aliases={7: 0},
        compiler_params=_params("arbitrary", "arbitrary", "arbitrary"),
        name="moe_ffn",
    )(idx.reshape(-1), gate.reshape(-1, 1), g.reshape(1, d), w_gate, w_up, w_down, h, acc)


def _even_layer(h, n_batch, seq, g_mix, w_in, w_out, s5, d_skip, w_glu, b_glu, pool_w, pool_scale):
    t = h.shape[0]
    z = norm_proj(h, g_mix, w_in.astype(BF16), F32)
    half = z.shape[1] // 2
    n_grp = half // S5_GROUP
    nc = t // S5_CHUNK
    u = z[:, :half].astype(BF16).reshape(nc, S5_CHUNK, n_grp, S5_GROUP)
    u = u.transpose(2, 0, 1, 3).reshape(n_grp, nc, S5_CHUNK * S5_GROUP)
    wt, wst, wout, al = s5_chunk_matrices(*s5)
    ys = s5_scan(u, wt.astype(BF16), wst.astype(BF16), wout.astype(BF16), al, n_batch)
    ys = ys.reshape(n_grp, nc, S5_CHUNK, S5_GROUP).transpose(1, 2, 0, 3).reshape(t, half)
    return even_post(z, ys, d_skip.astype(F32), w_glu.astype(BF16), b_glu.astype(F32), pool_w.astype(BF16),
                     pool_scale.astype(F32), w_out.astype(BF16), h, seq)


def _odd_layer(h, n_batch, layer, g_mix, cos, sin, w_qkv, w_out, q_gain, k_gain, lq1, lk1, lq2, lk2, subln_g):
    q, k, vt = attn_qkv(h, g_mix, w_qkv, cos, sin, q_gain, k_gain)
    lam_init = 0.8 - 0.6 * math.exp(-0.3 * layer)
    o = flash_diff_attention(q, k, vt, lq1, lk1, lq2, lk2, subln_g, n_batch, lam_init)
    return proj_residual(o, w_out.astype(BF16), h)


def _moe_layer(h, acc, n_batch, layer, g_ffn, w_router, w_gate_all, w_up_all, w_down_all):
    seq = h.shape[0] // n_batch
    cap = CAPACITY_FACTOR * seq // N_EXPERTS
    aff_t = router(h, g_ffn, w_router, n_batch)
    idx, gate = expert_select(aff_t, cap)
    return moe_ffn(h, acc, g_ffn, idx, gate, cast_layer_bf16(w_gate_all, layer), cast_layer_bf16(w_up_all, layer),
                   cast_layer_bf16(w_down_all, layer), n_batch)


def kernel(x, positions, norm_mix_g, norm_ffn_g, hyb_w_in, hyb_w_out, s5_lam_re, s5_lam_im, s5_log_dt, s5_b_re, s5_b_im, s5_c_re, s5_c_im, s5_d, s5_w_glu, s5_b_glu, pool_w, pool_scale, attn_w_qkv, attn_w_out, attn_q_norm_g, attn_k_norm_g, attn_lam_q1, attn_lam_k1, attn_lam_q2, attn_lam_k2, attn_subln_g, moe_w_router, moe_w_gate, moe_w_up, moe_w_down):
    n_batch, seq, d = x.shape
    depth = norm_mix_g.shape[0]
    h = x.reshape(n_batch * seq, d)
    cos, sin = rope_tables(positions)
    for layer in range(depth):
        if layer % 2 == 0:
            e = layer // 2
            s5 = (s5_lam_re[e], s5_lam_im[e], s5_log_dt[e], s5_b_re[e], s5_b_im[e], s5_c_re[e], s5_c_im[e])
            h, acc = _even_layer(h, n_batch, seq, norm_mix_g[layer], hyb_w_in[e], hyb_w_out[e], s5, s5_d[e],
                                 s5_w_glu[e], s5_b_glu[e], pool_w[e], pool_scale[e])
        else:
            o = layer // 2
            h, acc = _odd_layer(h, n_batch, layer, norm_mix_g[layer], cos, sin, attn_w_qkv[o], attn_w_out[o],
                                attn_q_norm_g[o], attn_k_norm_g[o], attn_lam_q1[o], attn_lam_k1[o],
                                attn_lam_q2[o], attn_lam_k2[o], attn_subln_g[o])
        h = _moe_layer(h, acc, n_batch, layer, norm_ffn_g[layer], moe_w_router[layer], moe_w_gate,
                       moe_w_up, moe_w_down)
    return h.reshape(n_batch, seq, d)
```
